```python
import jax, jax.numpy as jnp
from jax import lax
import numpy as np

D_MODEL = 4096
BATCH = 32
SEQ = 256
DEPTH = 2
DEC_BATCH = 8
DEC_SEQ = 4096
PAST_LEN = 512

GRID_W = 64
N_HEADS = 32
N_KV_HEADS = 8
HEAD_DIM = D_MODEL // N_HEADS
GROUP = N_HEADS // N_KV_HEADS
Q_DIM = N_HEADS * HEAD_DIM
KV_DIM = N_KV_HEADS * HEAD_DIM
QKV_DIM = Q_DIM + 2 * KV_DIM
D_FF = 11008
CONV_WIDTH = 3
WINDOW = 128
Q_BLOCK = 128
N_NEIGH = -(-WINDOW // Q_BLOCK)
ROPE_THETA = 10000.0
AXIS_DIM = HEAD_DIM // 2
N_MOD = 6
EPS = 1e-6
N_A_LAYERS = (DEPTH + 1) // 2
N_B_LAYERS = DEPTH // 2

kernel_name = 'hybrid_diffusion_prefix_trunk_step'


def rms_norm(x, gain):
    xf = x.astype(jnp.float32)
    y = xf * lax.rsqrt(jnp.mean(xf * xf, axis=-1, keepdims=True) + EPS)
    return (y * gain.astype(jnp.float32)).astype(x.dtype)


def modulate(h, shift, scale):
    return h * (1 + scale) + shift


def grid_positions(n_tokens):
    rows = n_tokens // GRID_W
    row = jnp.repeat(jnp.arange(rows, dtype=jnp.int32), GRID_W)
    col = jnp.tile(jnp.arange(GRID_W, dtype=jnp.int32), rows)
    return row, col


def rope_axis(x, pos):
    half = AXIS_DIM // 2
    inv_freq = ROPE_THETA ** (-jnp.arange(half, dtype=jnp.float32) / half)
    ang = pos.astype(jnp.float32)[:, None] * inv_freq[None, :]
    cos = jnp.cos(ang)[None, :, None, :]
    sin = jnp.sin(ang)[None, :, None, :]
    xf = x.astype(jnp.float32)
    x1, x2 = xf[..., :half], xf[..., half:]
    return jnp.concatenate([x1 * cos - x2 * sin, x1 * sin + x2 * cos], axis=-1).astype(x.dtype)


def axial_rope(x, row, col):
    return jnp.concatenate([rope_axis(x[..., :AXIS_DIM], row), rope_axis(x[..., AXIS_DIM:], col)], axis=-1)


def softmax_with_sink(s, sink):
    if sink is None:
        return jax.nn.softmax(s, axis=-1)
    sk = sink[None, :, :, None, None]
    m = jnp.maximum(jnp.max(s, axis=-1, keepdims=True), sk)
    e = jnp.exp(s - m)
    return e / (jnp.sum(e, axis=-1, keepdims=True) + jnp.exp(sk - m))


def split_qkv(h, w_qkv):
    B, T, _ = h.shape
    qkv = h @ w_qkv
    q = qkv[..., :Q_DIM].reshape(B, T, N_HEADS, HEAD_DIM)
    k = qkv[..., Q_DIM:Q_DIM + KV_DIM].reshape(B, T, N_KV_HEADS, HEAD_DIM)
    v = qkv[..., Q_DIM + KV_DIM:].reshape(B, T, N_KV_HEADS, HEAD_DIM)
    return q, k, v


def to_query_blocks(q):
    B, T = q.shape[:2]
    qb = q.reshape(B, T // Q_BLOCK, Q_BLOCK, N_KV_HEADS, GROUP, HEAD_DIM)
    return qb.transpose(1, 0, 2, 3, 4, 5)


def from_query_blocks(o, B, T):
    return o.transpose(1, 0, 2, 3, 4, 5).reshape(B, T, Q_DIM)


def dense_attention(q, k, v, sink):
    B, T = q.shape[:2]
    scale = HEAD_DIM ** -0.5

    def one(qi):
        s = jnp.einsum('bqgrd,bkgd->bgrqk', qi, k).astype(jnp.float32) * scale
        p = softmax_with_sink(s, sink).astype(v.dtype)
        return jnp.einsum('bgrqk,bkgd->bqgrd', p, v)

    return from_query_blocks(lax.map(one, to_query_blocks(q)), B, T)


def banded_attention_with_context(q, k, v, k_ctx, v_ctx, sink):
    B, T = q.shape[:2]
    n_blocks = T // Q_BLOCK
    pad = N_NEIGH * Q_BLOCK
    span = (2 * N_NEIGH + 1) * Q_BLOCK
    n_ctx = k_ctx.shape[1]
    scale = HEAD_DIM ** -0.5
    kp = jnp.pad(k, ((0, 0), (pad, pad), (0, 0), (0, 0)))
    vp = jnp.pad(v, ((0, 0), (pad, pad), (0, 0), (0, 0)))

    def one(args):
        i, qi = args
        start = i * Q_BLOCK
        kw = lax.dynamic_slice_in_dim(kp, start, span, axis=1)
        vw = lax.dynamic_slice_in_dim(vp, start, span, axis=1)
        qpos = start + jnp.arange(Q_BLOCK)
        kpos = start - pad + jnp.arange(span)
        valid = (jnp.abs(qpos[:, None] - kpos[None, :]) <= WINDOW) & (kpos >= 0)[None, :] & (kpos < T)[None, :]
        s_ctx = jnp.einsum('bqgrd,bkgd->bgrqk', qi, k_ctx).astype(jnp.float32) * scale
        s_win = jnp.einsum('bqgrd,bkgd->bgrqk', qi, kw).astype(jnp.float32) * scale
        s_win = jnp.where(valid, s_win, -jnp.inf)
        p = softmax_with_sink(jnp.concatenate([s_ctx, s_win], axis=-1), sink).astype(v.dtype)
        return (jnp.einsum('bgrqk,bkgd->bqgrd', p[..., :n_ctx], v_ctx)
                + jnp.einsum('bgrqk,bkgd->bqgrd', p[..., n_ctx:], vw))

    out = lax.map(one, (jnp.arange(n_blocks), to_query_blocks(q)))
    return from_query_blocks(out, B, T)


def mixer_a_context(h, w_qkv, w_o, sink):
    q, k, v = split_qkv(h, w_qkv)
    o = dense_attention(q, k, v, sink)
    return o @ w_o, k, v


def mixer_a_latent(h, w_qkv, w_o, sink, k_ctx, v_ctx, row, col):
    q, k, v = split_qkv(h, w_qkv)
    q = axial_rope(q, row, col)
    k = axial_rope(k, row, col)
    o = banded_attention_with_context(q, k, v, k_ctx, v_ctx, sink)
    return o @ w_o


def mixer_b_context(h, w_qkv, w_o, q_gain, k_gain):
    q, k, v = split_qkv(h, w_qkv)
    q = rms_norm(q, q_gain)
    k = rms_norm(k, k_gain)
    o = dense_attention(q, k, v, None)
    return o @ w_o, k, v


def mixer_b_latent(h, w_qkv, w_o, q_gain, k_gain, k_ctx, v_ctx, row, col):
    q, k, v = split_qkv(h, w_qkv)
    q = axial_rope(rms_norm(q, q_gain), row, col)
    k = axial_rope(rms_norm(k, k_gain), row, col)
    k_all = jnp.concatenate([k_ctx, k], axis=1)
    v_all = jnp.concatenate([v_ctx, v], axis=1)
    o = dense_attention(q, k_all, v_all, None)
    return o @ w_o


def conv_ffn(h, w_gate, w_up, w_down, conv_w, conv_b):
    a = h @ w_gate
    u = h @ w_up
    ap = jnp.pad(a, ((0, 0), (1, 1), (0, 0)))
    a = ap[:, :-2] * conv_w[0] + ap[:, 1:-1] * conv_w[1] + ap[:, 2:] * conv_w[2] + conv_b
    return (jax.nn.silu(a) * u) @ w_down


def setup_inputs(seed: int = 0) -> dict:
    key = jax.random.key(seed)
    ks = jax.random.split(key, 24)
    f = jnp.float32
    nrm = lambda k, shape, s: jax.random.normal(k, shape, f) * s
    return {
        'x_prompt': nrm(ks[0], (BATCH, SEQ, D_MODEL), 1.0),
        'x_sample': nrm(ks[1], (DEC_BATCH, DEC_SEQ, D_MODEL), 1.0),
        'cache_k': nrm(ks[2], (DEC_BATCH, DEPTH, PAST_LEN, N_KV_HEADS, HEAD_DIM), 1.0),
        'cache_v': nrm(ks[3], (DEC_BATCH, DEPTH, PAST_LEN, N_KV_HEADS, HEAD_DIM), 1.0),
        'c': nrm(ks[4], (DEC_BATCH, D_MODEL), 1.0),
        'c_ctx': nrm(ks[5], (D_MODEL,), 1.0),
        'w_mod': nrm(ks[6], (DEPTH, D_MODEL, N_MOD * D_MODEL), D_MODEL ** -0.5),
        'b_mod': nrm(ks[7], (DEPTH, N_MOD * D_MODEL), 0.02),
        'norm_attn': 1.0 + nrm(ks[8], (DEPTH, D_MODEL), 0.02),
        'norm_ffn': 1.0 + nrm(ks[9], (DEPTH, D_MODEL), 0.02),
        'w_qkv': nrm(ks[10], (DEPTH, D_MODEL, QKV_DIM), D_MODEL ** -0.5),
        'w_o': nrm(ks[11], (DEPTH, Q_DIM, D_MODEL), Q_DIM ** -0.5),
        'sink_a': nrm(ks[12], (N_A_LAYERS, N_HEADS), 0.5),
        'q_norm_b': 1.0 + nrm(ks[13], (N_B_LAYERS, HEAD_DIM), 0.02),
        'k_norm_b': 1.0 + nrm(ks[14], (N_B_LAYERS, HEAD_DIM), 0.02),
        'w_gate': nrm(ks[15], (DEPTH, D_MODEL, D_FF), D_MODEL ** -0.5),
        'w_up': nrm(ks[16], (DEPTH, D_MODEL, D_FF), D_MODEL ** -0.5),
        'w_down': nrm(ks[17], (DEPTH, D_FF, D_MODEL), D_FF ** -0.5),
        'conv_w': nrm(ks[18], (DEPTH, CONV_WIDTH, D_FF), CONV_WIDTH ** -0.5),
        'conv_b': nrm(ks[19], (DEPTH, D_FF), 0.02),
        'norm_f': 1.0 + nrm(ks[20], (D_MODEL,), 0.02),
    }


def reference(x_prompt, x_sample, cache_k, cache_v, c, c_ctx, w_mod, b_mod, norm_attn, norm_ffn,
              w_qkv, w_o, sink_a, q_norm_b, k_norm_b, w_gate, w_up, w_down, conv_w, conv_b, norm_f):
    xp, xs = x_prompt, x_sample
    row, col = grid_positions(xs.shape[1])
    cond_s = jax.nn.silu(c)
    cond_p = jax.nn.silu(c_ctx)
    ks_out, vs_out = [], []
    for i in range(DEPTH):
        mod_p = cond_p @ w_mod[i] + b_mod[i]
        mod_s = (cond_s @ w_mod[i] + b_mod[i])[:, None, :]
        sh_ap, sc_ap, g_ap, sh_fp, sc_fp, g_fp = jnp.split(mod_p, N_MOD, axis=-1)
        sh_as, sc_as, g_as, sh_fs, sc_fs, g_fs = jnp.split(mod_s, N_MOD, axis=-1)
        k_ctx_in, v_ctx_in = cache_k[:, i], cache_v[:, i]
        hp = modulate(rms_norm(xp, norm_attn[i]), sh_ap, sc_ap)
        hs = modulate(rms_norm(xs, norm_attn[i]), sh_as, sc_as)
        if i % 2 == 0:
            sink = sink_a[i // 2].astype(jnp.float32).reshape(N_KV_HEADS, GROUP)
            op, kc, vc = mixer_a_context(hp, w_qkv[i], w_o[i], sink)
            osm = mixer_a_latent(hs, w_qkv[i], w_o[i], sink, k_ctx_in, v_ctx_in, row, col)
        else:
            qg, kg = q_norm_b[i // 2], k_norm_b[i // 2]
            op, kc, vc = mixer_b_context(hp, w_qkv[i], w_o[i], qg, kg)
            osm = mixer_b_latent(hs, w_qkv[i], w_o[i], qg, kg, k_ctx_in, v_ctx_in, row, col)
        ks_out.append(kc)
        vs_out.append(vc)
        xp = xp + g_ap * op
        xs = xs + g_as * osm
        hp = modulate(rms_norm(xp, norm_ffn[i]), sh_fp, sc_fp)
        hs = modulate(rms_norm(xs, norm_ffn[i]), sh_fs, sc_fs)
        xp = xp + g_fp * conv_ffn(hp, w_gate[i], w_up[i], w_down[i], conv_w[i], conv_b[i])
        xs = xs + g_fs * conv_ffn(hs, w_gate[i], w_up[i], w_down[i], conv_w[i], conv_b[i])
    y_prompt = rms_norm(xp, norm_f)
    y_sample = rms_norm(xs, norm_f)
    ctx_k = jnp.stack(ks_out, axis=1)
    ctx_v = jnp.stack(vs_out, axis=1)
    return (y_prompt, y_sample, ctx_k, ctx_v)
```

```python
import functools

import jax
import jax.numpy as jnp
from jax import lax
from jax.experimental import pallas as pl
from jax.experimental.pallas import tpu as pltpu

GRID_W = 64
WINDOW = 128
Q_BLOCK = 128
ROPE_THETA = 10000.0
EPS = 1e-6
LANE = 128
SUB = 8
NEG_BIG = -1e30
VMEM_LIMIT_BYTES = 58 * 1024 * 1024

BF16 = jnp.bfloat16
F32 = jnp.float32

TILES = dict(
    mod_n=512,
    norm_m=256,
    qkv_m=1024, qkv_n=1024,
    proj_m=512, proj_n=1024,
    ffn_m=1024, ffn_f=512,
    down_m=512, down_n=256,
    attn_q=128, attn_kv=512,
)


def _tile(dim, pref, mult):
    if dim <= pref:
        return dim
    t = pref - pref % mult
    while t >= mult:
        if dim % t == 0:
            return t
        t -= mult
    return dim


def _params(*semantics):
    return pltpu.CompilerParams(dimension_semantics=semantics, vmem_limit_bytes=VMEM_LIMIT_BYTES)


def _silu(x):
    return x / (1.0 + jnp.exp(-x))


def _dot(a, b):
    return jnp.dot(a, b, preferred_element_type=F32)


def _dot_nt(a, b):
    return lax.dot_general(a, b, (((1,), (1,)), ((), ())), preferred_element_type=F32)


def _mod_kernel(c_ref, w_ref, b_ref, o_ref):
    a = _silu(c_ref[...]).astype(BF16)
    o_ref[...] = _dot(a, w_ref[...].astype(BF16)) + b_ref[...]


def _modulation(cond, w_mod, b_mod):
    depth, d, n = w_mod.shape
    rows = cond.shape[0]
    tn = _tile(n, TILES["mod_n"], LANE)
    return pl.pallas_call(
        _mod_kernel,
        grid=(depth, n // tn),
        in_specs=[
            pl.BlockSpec((rows, d), lambda l, j: (0, 0)),
            pl.BlockSpec((None, d, tn), lambda l, j: (l, 0, j)),
            pl.BlockSpec((None, 1, tn), lambda l, j: (l, 0, j)),
        ],
        out_specs=pl.BlockSpec((None, rows, tn), lambda l, j: (l, 0, j)),
        out_shape=jax.ShapeDtypeStruct((depth, rows, n), F32),
        compiler_params=_params("parallel", "parallel"),
        name="modulation",
    )(cond, w_mod, b_mod.reshape(depth, 1, n))


def _normed(x, gain):
    return x * lax.rsqrt(jnp.mean(x * x, axis=-1, keepdims=True) + EPS) * gain


def _norm_mod_kernel(x_ref, g_ref, sh_ref, sc_ref, o_ref):
    y = _normed(x_ref[...], g_ref[...])
    o_ref[...] = (y * (1.0 + sc_ref[...]) + sh_ref[...]).astype(o_ref.dtype)


def _norm_mod(x, gain, mod, shift_idx, rows_per_group):
    m, d = x.shape
    bm = _tile(min(m, rows_per_group), TILES["norm_m"], 8)
    grp = lambda i: (i * bm) // rows_per_group
    return pl.pallas_call(
        _norm_mod_kernel,
        grid=(m // bm,),
        in_specs=[
            pl.BlockSpec((bm, d), lambda i: (i, 0)),
            pl.BlockSpec((1, d), lambda i: (0, 0)),
            pl.BlockSpec((None, 1, d), lambda i: (grp(i), 0, shift_idx)),
            pl.BlockSpec((None, 1, d), lambda i: (grp(i), 0, shift_idx + 1)),
        ],
        out_specs=pl.BlockSpec((bm, d), lambda i: (i, 0)),
        out_shape=jax.ShapeDtypeStruct((m, d), BF16),
        compiler_params=_params("parallel"),
        name="norm_mod",
    )(x, gain.reshape(1, d), mod, mod)


def _norm_kernel(x_ref, g_ref, o_ref):
    o_ref[...] = _normed(x_ref[...], g_ref[...])


def _final_norm(x, gain):
    m, d = x.shape
    bm = _tile(m, TILES["norm_m"], 8)
    return pl.pallas_call(
        _norm_kernel,
        grid=(m // bm,),
        in_specs=[pl.BlockSpec((bm, d), lambda i: (i, 0)), pl.BlockSpec((1, d), lambda i: (0, 0))],
        out_specs=pl.BlockSpec((bm, d), lambda i: (i, 0)),
        out_shape=jax.ShapeDtypeStruct((m, d), F32),
        compiler_params=_params("parallel"),
        name="final_norm",
    )(x, gain.reshape(1, d))


def _qkv_kernel(*refs, n_q, n_k, norm, rope, want_kv32):
    h_ref, w_ref, cs_ref = refs[:3]
    refs = refs[3:]
    if rope:
        cos_ref, sin_ref = refs[:2]
        refs = refs[2:]
    o_ref = refs[0]
    kv_ref = refs[1] if want_kv32 else None
    j = pl.program_id(1)
    acc = _dot(h_ref[...], w_ref[...])
    bn = acc.shape[1]

    @pl.when(j < n_q + n_k)
    def _():
        if rope:
            cos = cos_ref[...]
            sin = sin_ref[...]
            lane = lax.broadcasted_iota(jnp.int32, (1, LANE), 1)
            first_half = (lane % (LANE // 2)) < (LANE // 4)
        outs = []
        for hh in range(bn // LANE):
            blk = acc[:, hh * LANE:(hh + 1) * LANE]
            if norm:
                blk = blk * lax.rsqrt(jnp.mean(blk * blk, axis=-1, keepdims=True) + EPS)
            blk = blk * cs_ref[:, hh * LANE:(hh + 1) * LANE]
            if rope:
                partner = jnp.where(first_half, pltpu.roll(blk, LANE - LANE // 4, 1),
                                    pltpu.roll(blk, LANE // 4, 1))
                blk = blk * cos + partner * sin
            outs.append(blk)
        res = jnp.concatenate(outs, axis=1) if len(outs) > 1 else outs[0]
        o_ref[...] = res.astype(o_ref.dtype)
        if want_kv32:
            @pl.when(j >= n_q)
            def _():
                kv_ref[...] = res

    @pl.when(j >= n_q + n_k)
    def _():
        o_ref[...] = acc.astype(o_ref.dtype)
        if want_kv32:
            kv_ref[...] = acc


def _qkv_proj(h, w, colscale, q_dim, kv_dim, *, norm, rope_tables, want_kv32):
    m, d = h.shape
    n = w.shape[1]
    bm = _tile(m, TILES["qkv_m"], 16)
    bn = _tile(kv_dim, TILES["qkv_n"], LANE)
    assert q_dim % bn == 0 and kv_dim % bn == 0
    n_q, n_k = q_dim // bn, kv_dim // bn
    rope = rope_tables is not None
    in_specs = [
        pl.BlockSpec((bm, d), lambda i, j: (i, 0)),
        pl.BlockSpec((d, bn), lambda i, j: (0, j)),
        pl.BlockSpec((1, bn), lambda i, j: (0, j)),
    ]
    args = [h, w, colscale]
    if rope:
        cos, sin = rope_tables
        t = cos.shape[0]
        assert t % bm == 0
        per = t // bm
        in_specs += [pl.BlockSpec((bm, LANE), lambda i, j: (i % per, 0))] * 2
        args += [cos, sin]
    out_specs = [pl.BlockSpec((bm, bn), lambda i, j: (i, j))]
    out_shape = [jax.ShapeDtypeStruct((m, n), BF16)]
    if want_kv32:
        out_specs.append(pl.BlockSpec((bm, bn), lambda i, j: (i, jnp.maximum(j - n_q, 0))))
        out_shape.append(jax.ShapeDtypeStruct((m, 2 * kv_dim), F32))
    return pl.pallas_call(
        functools.partial(_qkv_kernel, n_q=n_q, n_k=n_k, norm=norm, rope=rope, want_kv32=want_kv32),
        grid=(m // bm, n // bn),
        in_specs=in_specs,
        out_specs=out_specs,
        out_shape=out_shape,
        compiler_params=_params("parallel", "arbitrary"),
        name="qkv_proj",
    )(*args)


def _stack_heads(q_ref, r_heads):
    return jnp.concatenate([q_ref[:, r * LANE:(r + 1) * LANE] for r in range(r_heads)], axis=0)


def _unstack_heads(o_ref, o, r_heads):
    tq = o.shape[0] // r_heads
    for r in range(r_heads):
        o_ref[:, r * LANE:(r + 1) * LANE] = o[r * tq:(r + 1) * tq].astype(o_ref.dtype)


def _sink_column(sink_ref, g, r_heads, tq):
    return jnp.concatenate(
        [jnp.full((tq, 1), sink_ref[g * r_heads + r], F32) for r in range(r_heads)], axis=0)


def _ctx_attn_kernel(sink_ref, q_ref, k_ref, v_ref, o_ref, *, r_heads, has_sink):
    g = pl.program_id(1)
    tq = q_ref.shape[0]
    qs = _stack_heads(q_ref, r_heads)
    s = _dot_nt(qs, k_ref[...])
    m = jnp.max(s, axis=-1, keepdims=True)
    if has_sink:
        sk = _sink_column(sink_ref, g, r_heads, tq)
        m = jnp.maximum(m, sk)
    e = jnp.exp(s - m)
    den = jnp.sum(e, axis=-1, keepdims=True)
    if has_sink:
        den = den + jnp.exp(sk - m)
    p = (e * (1.0 / den)).astype(BF16)
    _unstack_heads(o_ref, _dot(p, v_ref[...]), r_heads)


def _ctx_attention(qkv, sink, batch, q_dim, kv_dim, *, has_sink):
    m = qkv.shape[0]
    t = m // batch
    n_kv = kv_dim // LANE
    r_heads = q_dim // kv_dim
    qw = r_heads * LANE
    return pl.pallas_call(
        functools.partial(_ctx_attn_kernel, r_heads=r_heads, has_sink=has_sink),
        grid=(batch, n_kv),
        in_specs=[
            pl.BlockSpec(memory_space=pltpu.SMEM),
            pl.BlockSpec((t, qw), lambda b, g: (b, g)),
            pl.BlockSpec((t, LANE), lambda b, g: (b, q_dim // LANE + g)),
            pl.BlockSpec((t, LANE), lambda b, g: (b, (q_dim + kv_dim) // LANE + g)),
        ],
        out_specs=pl.BlockSpec((t, qw), lambda b, g: (b, g)),
        out_shape=jax.ShapeDtypeStruct((m, q_dim), BF16),
        compiler_params=_params("parallel", "parallel"),
        name="ctx_attention",
    )(sink, qkv, qkv, qkv)


def _win_attn_kernel(sink_ref, q_ref, k_ref, v_ref, kc_ref, vc_ref, o_ref, *, r_heads, span):
    g = pl.program_id(1)
    i = pl.program_id(2)
    tq = q_ref.shape[0]
    t = k_ref.shape[0]
    base = pl.multiple_of(jnp.clip(i * tq - WINDOW, 0, t - span), LANE)
    kw = k_ref[pl.ds(base, span), :]
    vw = v_ref[pl.ds(base, span), :]
    qpos = i * tq + lax.broadcasted_iota(jnp.int32, (r_heads * tq, span), 0) % tq
    kpos = base + lax.broadcasted_iota(jnp.int32, (r_heads * tq, span), 1)
    valid = jnp.abs(qpos - kpos) <= WINDOW
    qs = _stack_heads(q_ref, r_heads)
    s_c = _dot_nt(qs, kc_ref[...])
    s_w = jnp.where(valid, _dot_nt(qs, kw), NEG_BIG)
    sk = _sink_column(sink_ref, g, r_heads, tq)
    m = jnp.maximum(jnp.maximum(jnp.max(s_c, axis=-1, keepdims=True),
                                jnp.max(s_w, axis=-1, keepdims=True)), sk)
    e_c = jnp.exp(s_c - m)
    e_w = jnp.exp(s_w - m)
    den = (jnp.sum(e_c, axis=-1, keepdims=True) + jnp.sum(e_w, axis=-1, keepdims=True)
           + jnp.exp(sk - m))
    inv = 1.0 / den
    o = _dot((e_c * inv).astype(BF16), vc_ref[...]) + _dot((e_w * inv).astype(BF16), vw)
    _unstack_heads(o_ref, o, r_heads)


def _window_attention(qkv, k_ctx, v_ctx, sink, batch, q_dim, kv_dim):
    m = qkv.shape[0]
    t = m // batch
    n_kv = kv_dim // LANE
    r_heads = q_dim // kv_dim
    qw = r_heads * LANE
    tq = Q_BLOCK
    span = tq + 2 * (-(-WINDOW // LANE) * LANE)
    assert t % tq == 0 and t >= span
    nb = t // tq
    p = k_ctx.shape[2]
    return pl.pallas_call(
        functools.partial(_win_attn_kernel, r_heads=r_heads, span=span),
        grid=(batch, n_kv, nb),
        in_specs=[
            pl.BlockSpec(memory_space=pltpu.SMEM),
            pl.BlockSpec((tq, qw), lambda b, g, i: (b * nb + i, g)),
            pl.BlockSpec((t, LANE), lambda b, g, i: (b, q_dim // LANE + g)),
            pl.BlockSpec((t, LANE), lambda b, g, i: (b, (q_dim + kv_dim) // LANE + g)),
            pl.BlockSpec((None, None, p, LANE), lambda b, g, i: (b, g, 0, 0)),
            pl.BlockSpec((None, None, p, LANE), lambda b, g, i: (b, g, 0, 0)),
        ],
        out_specs=pl.BlockSpec((tq, qw), lambda b, g, i: (b * nb + i, g)),
        out_shape=jax.ShapeDtypeStruct((m, q_dim), BF16),
        compiler_params=_params("parallel", "parallel", "arbitrary"),
        name="window_attention",
    )(sink, qkv, qkv, qkv, k_ctx, v_ctx)


def _dense_attn_kernel(q_ref, k_ref, v_ref, kc_ref, vc_ref, o_ref, *, r_heads, kv_chunk):
    t = k_ref.shape[0]
    qs = _stack_heads(q_ref, r_heads)
    rows = qs.shape[0]

    def step(kc, vc, carry):
        m, l, acc = carry
        s = _dot_nt(qs, kc)
        m_new = jnp.maximum(m, jnp.max(s, axis=-1, keepdims=True))
        alpha = jnp.exp(m - m_new)
        p = jnp.exp(s - m_new)
        l = alpha * l + jnp.sum(p, axis=-1, keepdims=True)
        acc = alpha * acc + _dot(p.astype(BF16), vc)
        return m_new, l, acc

    carry = (jnp.full((rows, 1), NEG_BIG, F32), jnp.zeros((rows, 1), F32), jnp.zeros((rows, LANE), F32))
    carry = step(kc_ref[...], vc_ref[...], carry)

    def body(c, carry):
        start = pl.multiple_of(c * kv_chunk, kv_chunk)
        return step(k_ref[pl.ds(start, kv_chunk), :], v_ref[pl.ds(start, kv_chunk), :], carry)

    m, l, acc = lax.fori_loop(0, t // kv_chunk, body, carry)
    _unstack_heads(o_ref, acc * (1.0 / l), r_heads)


def _dense_attention(qkv, k_ctx, v_ctx, batch, q_dim, kv_dim):
    m = qkv.shape[0]
    t = m // batch
    n_kv = kv_dim // LANE
    r_heads = q_dim // kv_dim
    qw = r_heads * LANE
    tq = _tile(t, TILES["attn_q"], 16)
    kv_chunk = _tile(t, TILES["attn_kv"], LANE)
    nb = t // tq
    p = k_ctx.shape[2]
    return pl.pallas_call(
        functools.partial(_dense_attn_kernel, r_heads=r_heads, kv_chunk=kv_chunk),
        grid=(batch, n_kv, nb),
        in_specs=[
            pl.BlockSpec((tq, qw), lambda b, g, i: (b * nb + i, g)),
            pl.BlockSpec((t, LANE), lambda b, g, i: (b, q_dim // LANE + g)),
            pl.BlockSpec((t, LANE), lambda b, g, i: (b, (q_dim + kv_dim) // LANE + g)),
            pl.BlockSpec((None, None, p, LANE), lambda b, g, i: (b, g, 0, 0)),
            pl.BlockSpec((None, None, p, LANE), lambda b, g, i: (b, g, 0, 0)),
        ],
        out_specs=pl.BlockSpec((tq, qw), lambda b, g, i: (b * nb + i, g)),
        out_shape=jax.ShapeDtypeStruct((m, q_dim), BF16),
        compiler_params=_params("parallel", "parallel", "arbitrary"),
        name="dense_attention",
    )(qkv, qkv, qkv, k_ctx, v_ctx)


def _proj_res_kernel(a_ref, w_ref, x_ref, g_ref, o_ref):
    o_ref[...] = x_ref[...] + g_ref[...] * _dot(a_ref[...], w_ref[...])


def _proj_residual(a, w, x, mod, gate_idx, rows_per_group, bm_pref, bn_pref):
    m, k = a.shape
    d = w.shape[1]
    bm = _tile(min(m, rows_per_group), bm_pref, 16)
    bn = _tile(d, bn_pref, LANE)
    nj = d // bn
    grp = lambda i: (i * bm) // rows_per_group
    return pl.pallas_call(
        _proj_res_kernel,
        grid=(m // bm, nj),
        in_specs=[
            pl.BlockSpec((bm, k), lambda i, j: (i, 0)),
            pl.BlockSpec((k, bn), lambda i, j: (0, j)),
            pl.BlockSpec((bm, bn), lambda i, j: (i, j)),
            pl.BlockSpec((None, 1, bn), lambda i, j: (grp(i), 0, gate_idx * nj + j)),
        ],
        out_specs=pl.BlockSpec((bm, bn), lambda i, j: (i, j)),
        out_shape=jax.ShapeDtypeStruct((m, d), F32),
        compiler_params=_params("parallel", "parallel"),
        name="proj_residual",
    )(a, w, x, mod)


def _matmul_kernel(a_ref, w_ref, o_ref):
    o_ref[...] = _dot(a_ref[...], w_ref[...])


def _edge_gate(h, wg, bm):
    m, d = h.shape
    nt = m // bm
    fp = wg.shape[1]
    last = jnp.roll(h[bm - 1::bm], 1, axis=0)
    first = jnp.roll(h[0::bm], -1, axis=0)
    edge = jnp.zeros((nt, 2 * SUB, d), h.dtype).at[:, SUB - 1].set(last).at[:, SUB].set(first)
    bn = _tile(fp, TILES["ffn_f"], LANE)
    out = pl.pallas_call(
        _matmul_kernel,
        grid=(fp // bn,),
        in_specs=[pl.BlockSpec((nt * 2 * SUB, d), lambda j: (0, 0)), pl.BlockSpec((d, bn), lambda j: (0, j))],
        out_specs=pl.BlockSpec((nt * 2 * SUB, bn), lambda j: (0, j)),
        out_shape=jax.ShapeDtypeStruct((nt * 2 * SUB, fp), F32),
        compiler_params=_params("parallel"),
        name="edge_gate",
    )(edge.reshape(nt * 2 * SUB, d), wg)
    return out.reshape(nt, 2 * SUB, fp)


def _gate_up_kernel(h_ref, wg_ref, wu_ref, cw_ref, cb_ref, e_ref, o_ref, a_ref, *, bm, seq_len):
    a_ref[0:SUB, :] = e_ref[0:SUB, :]
    a_ref[SUB:SUB + bm, :] = _dot(h_ref[...], wg_ref[...])
    a_ref[SUB + bm:, :] = e_ref[SUB:, :]
    u = _dot(h_ref[...], wu_ref[...])
    row = pl.program_id(0) * bm + lax.broadcasted_iota(jnp.int32, (bm, 1), 0)
    pos = row % seq_len
    prev = jnp.where(pos == 0, 0.0, a_ref[SUB - 1:SUB - 1 + bm, :])
    nxt = jnp.where(pos == seq_len - 1, 0.0, a_ref[SUB + 1:SUB + 1 + bm, :])
    cur = a_ref[SUB:SUB + bm, :]
    a = prev * cw_ref[0:1, :] + cur * cw_ref[1:2, :] + nxt * cw_ref[2:3, :] + cb_ref[...]
    o_ref[...] = (_silu(a) * u).astype(o_ref.dtype)


def _gate_up(h, wg, wu, conv_w, conv_b, seq_len, bm):
    m, d = h.shape
    fp = wg.shape[1]
    bf = _tile(fp, TILES["ffn_f"], LANE)
    edge = _edge_gate(h, wg, bm)
    return pl.pallas_call(
        functools.partial(_gate_up_kernel, bm=bm, seq_len=seq_len),
        grid=(m // bm, fp // bf),
        in_specs=[
            pl.BlockSpec((bm, d), lambda i, f: (i, 0)),
            pl.BlockSpec((d, bf), lambda i, f: (0, f)),
            pl.BlockSpec((d, bf), lambda i, f: (0, f)),
            pl.BlockSpec((conv_w.shape[0], bf), lambda i, f: (0, f)),
            pl.BlockSpec((1, bf), lambda i, f: (0, f)),
            pl.BlockSpec((None, 2 * SUB, bf), lambda i, f: (i, 0, f)),
        ],
        out_specs=pl.BlockSpec((bm, bf), lambda i, f: (i, f)),
        out_shape=jax.ShapeDtypeStruct((m, fp), BF16),
        scratch_shapes=[pltpu.VMEM((bm + 2 * SUB, bf), F32)],
        compiler_params=_params("parallel", "parallel"),
        name="gate_up",
    )(h, wg, wu, conv_w, conv_b, edge)


def _rope_tables(n_tokens):
    axis_dim = LANE // 2
    half = axis_dim // 2
    rows = n_tokens // GRID_W
    row = jnp.repeat(jnp.arange(rows, dtype=jnp.int32), GRID_W)
    col = jnp.tile(jnp.arange(GRID_W, dtype=jnp.int32), rows)
    inv_freq = ROPE_THETA ** (-jnp.arange(half, dtype=F32) / half)
    cos_parts, sin_parts = [], []
    for pos in (row, col):
        ang = pos.astype(F32)[:, None] * inv_freq[None, :]
        c, s = jnp.cos(ang), jnp.sin(ang)
        cos_parts += [c, c]
        sin_parts += [-s, s]
    return jnp.concatenate(cos_parts, axis=1), jnp.concatenate(sin_parts, axis=1)


def _pad_cols(a, n):
    return jnp.pad(a, ((0, 0), (0, n - a.shape[1])))


def kernel(x_prompt, x_sample, cache_k, cache_v, c, c_ctx, w_mod, b_mod, norm_attn, norm_ffn, w_qkv, w_o,
           sink_a, q_norm_b, k_norm_b, w_gate, w_up, w_down, conv_w, conv_b, norm_f):
    batch_p, seq_p, d = x_prompt.shape
    batch_s, seq_s, _ = x_sample.shape
    depth = w_mod.shape[0]
    n_kv, head_dim = cache_k.shape[3], cache_k.shape[4]
    n_heads = sink_a.shape[1]
    assert head_dim == LANE
    q_dim, kv_dim = n_heads * head_dim, n_kv * head_dim
    d_ff = w_gate.shape[2]
    scale = head_dim ** -0.5

    ffn_m_p = _tile(batch_p * seq_p, TILES["ffn_m"], seq_p)
    ffn_m_s = _tile(seq_s, TILES["ffn_m"], 16)
    bf = min(TILES["ffn_f"], -(-d_ff // LANE) * LANE)
    fp = -(-d_ff // bf) * bf

    xp = x_prompt.reshape(batch_p * seq_p, d)
    xs = x_sample.reshape(batch_s * seq_s, d)

    n_cond = -(-(batch_s + 1) // 8) * 8
    cond = jnp.zeros((n_cond, d), F32).at[:batch_s].set(c).at[batch_s].set(c_ctx)
    mod = _modulation(cond, w_mod, b_mod)

    rope_tables = _rope_tables(seq_s)
    ones_kv = jnp.ones((kv_dim,), F32)
    ks_out, vs_out = [], []
    for i in range(depth):
        mod_s = mod[i, :batch_s].reshape(batch_s, 1, -1)
        mod_p = mod[i, batch_s:batch_s + 1].reshape(1, 1, -1)
        wqkv = w_qkv[i].astype(BF16)
        wo = w_o[i].astype(BF16)
        wg = _pad_cols(w_gate[i].astype(BF16), fp)
        wu = _pad_cols(w_up[i].astype(BF16), fp)
        wd = jnp.pad(w_down[i].astype(BF16), ((0, fp - d_ff), (0, 0)))
        cw = _pad_cols(conv_w[i], fp)
        cb = _pad_cols(conv_b[i].reshape(1, d_ff), fp)
        k_ctx = cache_k[:, i].transpose(0, 2, 1, 3).astype(BF16)
        v_ctx = cache_v[:, i].transpose(0, 2, 1, 3).astype(BF16)
        mixer_a = i % 2 == 0
        if mixer_a:
            sink = sink_a[i // 2].astype(F32)
            colscale = jnp.concatenate([jnp.full((q_dim,), scale, F32), ones_kv, ones_kv])
        else:
            sink = jnp.zeros((n_heads,), F32)
            colscale = jnp.concatenate([jnp.tile(q_norm_b[i // 2].astype(F32), n_heads) * scale,
                                        jnp.tile(k_norm_b[i // 2].astype(F32), n_kv), ones_kv])
        colscale = colscale.reshape(1, -1)

        hp = _norm_mod(xp, norm_attn[i], mod_p, 0, batch_p * seq_p)
        hs = _norm_mod(xs, norm_attn[i], mod_s, 0, seq_s)
        qkv_p, kv32 = _qkv_proj(hp, wqkv, colscale, q_dim, kv_dim, norm=not mixer_a, rope_tables=None,
                                want_kv32=True)
        qkv_s, = _qkv_proj(hs, wqkv, colscale, q_dim, kv_dim, norm=not mixer_a, rope_tables=rope_tables,
                           want_kv32=False)
        ks_out.append(kv32[:, :kv_dim].reshape(batch_p, seq_p, n_kv, head_dim))
        vs_out.append(kv32[:, kv_dim:].reshape(batch_p, seq_p, n_kv, head_dim))
        op = _ctx_attention(qkv_p, sink, batch_p, q_dim, kv_dim, has_sink=mixer_a)
        if mixer_a:
            osm = _window_attention(qkv_s, k_ctx, v_ctx, sink, batch_s, q_dim, kv_dim)
        else:
            osm = _dense_attention(qkv_s, k_ctx, v_ctx, batch_s, q_dim, kv_dim)
        xp = _proj_residual(op, wo, xp, mod_p, 2, batch_p * seq_p, TILES["proj_m"], TILES["proj_n"])
        xs = _proj_residual(osm, wo, xs, mod_s, 2, seq_s, TILES["proj_m"], TILES["proj_n"])

        hp = _norm_mod(xp, norm_ffn[i], mod_p, 3, batch_p * seq_p)
        hs = _norm_mod(xs, norm_ffn[i], mod_s, 3, seq_s)
        zp = _gate_up(hp, wg, wu, cw, cb, seq_p, ffn_m_p)
        zs = _gate_up(hs, wg, wu, cw, cb, seq_s, ffn_m_s)
        xp = _proj_residual(zp, wd, xp, mod_p, 5, batch_p * seq_p, TILES["down_m"], TILES["down_n"])
        xs = _proj_residual(zs, wd, xs, mod_s, 5, seq_s, TILES["down_m"], TILES["down_n"])

    y_prompt = _final_norm(xp, norm_f).reshape(batch_p, seq_p, d)
    y_sample = _final_norm(xs, norm_f).reshape(batch_s, seq_s, d)
    return y_prompt, y_sample, jnp.stack(ks_out, axis=1), jnp.stack(vs_out, axis=1)
```

```python
import functools
import math

import jax
import jax.numpy as jnp
from jax import lax
from jax.experimental import pallas as pl
from jax.experimental.pallas import tpu as pltpu

GRID_W = 64
WINDOW = 128
ROPE_THETA = 10000.0
EPS = 1e-6
LANE = 128
SUB = 8
NEG_BIG = -1e30
UNDERFLOW_SUM = 2.0 ** -60
RETRY_SHIFT = 64.0
LOG2E = math.log2(math.e)
VMEM_LIMIT_BYTES = 58 * 1024 * 1024

BF16 = jnp.bfloat16
F32 = jnp.float32

TILES = dict(
    mod_n=512,
    norm_m=256,
    qkv_m=1024, qkv_n=1024,
    proj_m=512, proj_n=1024,
    ffn_m=1024, ffn_f=512,
    down_m=512, down_n=256,
    win_q=256,
    attn_q=128, attn_kv=512,
)


def _tile(dim, pref, mult):
    if dim <= pref:
        return dim
    t = pref - pref % mult
    while t >= mult:
        if dim % t == 0:
            return t
        t -= mult
    return dim


def _params(*semantics):
    return pltpu.CompilerParams(dimension_semantics=semantics, vmem_limit_bytes=VMEM_LIMIT_BYTES)


def _silu(x):
    return x / (1.0 + jnp.exp(-x))


def _dot(a, b):
    return jnp.dot(a, b, preferred_element_type=F32)


def _dot_nt(a, b):
    return lax.dot_general(a, b, (((1,), (1,)), ((), ())), preferred_element_type=F32)


def _mod_kernel(c_ref, w_ref, b_ref, o_ref):
    a = _silu(c_ref[...]).astype(BF16)
    o_ref[...] = _dot(a, w_ref[...].astype(BF16)) + b_ref[...]


def _modulation(cond, w_mod, b_mod):
    depth, d, n = w_mod.shape
    rows = cond.shape[0]
    tn = _tile(n, TILES["mod_n"], LANE)
    return pl.pallas_call(
        _mod_kernel,
        grid=(depth, n // tn),
        in_specs=[
            pl.BlockSpec((rows, d), lambda l, j: (0, 0)),
            pl.BlockSpec((None, d, tn), lambda l, j: (l, 0, j)),
            pl.BlockSpec((None, 1, tn), lambda l, j: (l, 0, j)),
        ],
        out_specs=pl.BlockSpec((None, rows, tn), lambda l, j: (l, 0, j)),
        out_shape=jax.ShapeDtypeStruct((depth, rows, n), F32),
        compiler_params=_params("parallel", "parallel"),
        name="modulation",
    )(cond, w_mod, b_mod.reshape(depth, 1, n))


class _Mod:
    def __init__(self, table, base, rows_per_group):
        self.table, self.base, self.rows_per_group = table, base, rows_per_group

    def spec(self, slot, bm, bn, nj):
        base, rpg = self.base, self.rows_per_group
        if nj == 1:
            return pl.BlockSpec((None, 1, bn), lambda i, *_: (base + (i * bm) // rpg, 0, slot))
        return pl.BlockSpec((None, 1, bn), lambda i, j: (base + (i * bm) // rpg, 0, slot * nj + j))


def _normed(x, gain):
    return x * lax.rsqrt(jnp.mean(x * x, axis=-1, keepdims=True) + EPS) * gain


def _norm_mod_kernel(x_ref, g_ref, sh_ref, sc_ref, o_ref):
    y = _normed(x_ref[...], g_ref[...])
    o_ref[...] = (y * (1.0 + sc_ref[...]) + sh_ref[...]).astype(o_ref.dtype)


def _norm_mod(x, gains, layer, mod, shift_slot):
    m, d = x.shape
    bm = _tile(min(m, mod.rows_per_group), TILES["norm_m"], SUB)
    return pl.pallas_call(
        _norm_mod_kernel,
        grid=(m // bm,),
        in_specs=[
            pl.BlockSpec((bm, d), lambda i: (i, 0)),
            pl.BlockSpec((None, 1, d), lambda i: (layer, 0, 0)),
            mod.spec(shift_slot, bm, d, 1),
            mod.spec(shift_slot + 1, bm, d, 1),
        ],
        out_specs=pl.BlockSpec((bm, d), lambda i: (i, 0)),
        out_shape=jax.ShapeDtypeStruct((m, d), BF16),
        compiler_params=_params("parallel"),
        name="norm_mod",
    )(x, gains, mod.table, mod.table)


def _norm_kernel(x_ref, g_ref, o_ref):
    o_ref[...] = _normed(x_ref[...], g_ref[...])


def _final_norm(x, gain):
    m, d = x.shape
    bm = _tile(m, TILES["norm_m"], SUB)
    return pl.pallas_call(
        _norm_kernel,
        grid=(m // bm,),
        in_specs=[pl.BlockSpec((bm, d), lambda i: (i, 0)), pl.BlockSpec((1, d), lambda i: (0, 0))],
        out_specs=pl.BlockSpec((bm, d), lambda i: (i, 0)),
        out_shape=jax.ShapeDtypeStruct((m, d), F32),
        compiler_params=_params("parallel"),
        name="final_norm",
    )(x, gain.reshape(1, d))


def _qkv_kernel(*refs, n_q, n_k, norm, rope, want_kv32):
    h_ref, w_ref, cs_ref = refs[:3]
    refs = refs[3:]
    if rope:
        cos_ref, sin_ref = refs[:2]
        refs = refs[2:]
    o_ref = refs[0]
    kv_ref = refs[1] if want_kv32 else None
    j = pl.program_id(1)
    acc = _dot(h_ref[...], w_ref[...])
    bn = acc.shape[1]

    @pl.when(j < n_q + n_k)
    def _():
        if rope:
            cos = cos_ref[...]
            sin = sin_ref[...]
            lane = lax.broadcasted_iota(jnp.int32, (1, LANE), 1)
            first_half = (lane % (LANE // 2)) < (LANE // 4)
        outs = []
        for hh in range(bn // LANE):
            blk = acc[:, hh * LANE:(hh + 1) * LANE]
            if norm:
                blk = blk * lax.rsqrt(jnp.mean(blk * blk, axis=-1, keepdims=True) + EPS)
            blk = blk * cs_ref[:, hh * LANE:(hh + 1) * LANE]
            if rope:
                partner = jnp.where(first_half, pltpu.roll(blk, LANE - LANE // 4, 1),
                                    pltpu.roll(blk, LANE // 4, 1))
                blk = blk * cos + partner * sin
            outs.append(blk)
        res = jnp.concatenate(outs, axis=1) if len(outs) > 1 else outs[0]
        o_ref[...] = res.astype(o_ref.dtype)
        if want_kv32:
            @pl.when(j >= n_q)
            def _():
                kv_ref[...] = res

    @pl.when(j >= n_q + n_k)
    def _():
        o_ref[...] = acc.astype(o_ref.dtype)
        if want_kv32:
            kv_ref[...] = acc


def _qkv_proj(h, w, layer, colscale, q_dim, kv_dim, *, norm, rope_tables, want_kv32):
    m, d = h.shape
    n = w.shape[2]
    bm = _tile(m, TILES["qkv_m"], 16)
    bn = _tile(kv_dim, TILES["qkv_n"], LANE)
    assert q_dim % bn == 0 and kv_dim % bn == 0
    n_q, n_k = q_dim // bn, kv_dim // bn
    rope = rope_tables is not None
    in_specs = [
        pl.BlockSpec((bm, d), lambda i, j: (i, 0)),
        pl.BlockSpec((None, d, bn), lambda i, j: (layer, 0, j)),
        pl.BlockSpec((1, bn), lambda i, j: (0, j)),
    ]
    args = [h, w, colscale]
    if rope:
        cos, sin = rope_tables
        t = cos.shape[0]
        assert t % bm == 0
        per = t // bm
        in_specs += [pl.BlockSpec((bm, LANE), lambda i, j: (i % per, 0))] * 2
        args += [cos, sin]
    out_specs = [pl.BlockSpec((bm, bn), lambda i, j: (i, j))]
    out_shape = [jax.ShapeDtypeStruct((m, n), BF16)]
    if want_kv32:
        out_specs.append(pl.BlockSpec((bm, bn), lambda i, j: (i, jnp.maximum(j - n_q, 0))))
        out_shape.append(jax.ShapeDtypeStruct((m, 2 * kv_dim), F32))
    return pl.pallas_call(
        functools.partial(_qkv_kernel, n_q=n_q, n_k=n_k, norm=norm, rope=rope, want_kv32=want_kv32),
        grid=(m // bm, n // bn),
        in_specs=in_specs,
        out_specs=out_specs,
        out_shape=out_shape,
        compiler_params=_params("parallel", "arbitrary"),
        name="qkv_proj",
    )(*args)


def _stack_heads(q_ref, r_heads):
    return jnp.concatenate([q_ref[:, r * LANE:(r + 1) * LANE] for r in range(r_heads)], axis=0)


def _ctx_attn_kernel(sink_ref, q_ref, k_ref, v_ref, o_ref, *, r_heads, has_sink):
    g = pl.program_id(1)
    tq = q_ref.shape[0]
    ex = jnp.exp if has_sink else jnp.exp2
    qs = _stack_heads(q_ref, r_heads)
    s = _dot_nt(qs, k_ref[...])
    m = jnp.max(s, axis=-1, keepdims=True)
    if has_sink:
        sk = jnp.concatenate(
            [jnp.full((tq, 1), sink_ref[g * r_heads + r], F32) for r in range(r_heads)], axis=0)
        m = jnp.maximum(m, sk)
    e = ex(s - m)
    den = jnp.sum(e, axis=-1, keepdims=True)
    if has_sink:
        den = den + ex(sk - m)
    p = (e * (1.0 / den)).astype(BF16)
    o = _dot(p, v_ref[...])
    for r in range(r_heads):
        o_ref[:, r * LANE:(r + 1) * LANE] = o[r * tq:(r + 1) * tq].astype(o_ref.dtype)


def _ctx_attention(qkv, sink, batch, q_dim, kv_dim, *, has_sink):
    m = qkv.shape[0]
    t = m // batch
    n_kv = kv_dim // LANE
    r_heads = q_dim // kv_dim
    qw = r_heads * LANE
    return pl.pallas_call(
        functools.partial(_ctx_attn_kernel, r_heads=r_heads, has_sink=has_sink),
        grid=(batch, n_kv),
        in_specs=[
            pl.BlockSpec(memory_space=pltpu.SMEM),
            pl.BlockSpec((t, qw), lambda b, g: (b, g)),
            pl.BlockSpec((t, LANE), lambda b, g: (b, q_dim // LANE + g)),
            pl.BlockSpec((t, LANE), lambda b, g: (b, (q_dim + kv_dim) // LANE + g)),
        ],
        out_specs=pl.BlockSpec((t, qw), lambda b, g: (b, g)),
        out_shape=jax.ShapeDtypeStruct((m, q_dim), BF16),
        compiler_params=_params("parallel", "parallel"),
        name="ctx_attention",
    )(sink, qkv, qkv, qkv)


def _store_heads_t(o_ref, o_t, r_heads):
    tq = o_t.shape[1] // r_heads
    for r in range(r_heads):
        o_ref[:, r * LANE:(r + 1) * LANE] = o_t[:, r * tq:(r + 1) * tq].T.astype(o_ref.dtype)


def _win_attn_kernel(sink_ref, q_ref, k_ref, vt_ref, kc_ref, vct_ref, o_ref, *, r_heads, span):
    g = pl.program_id(1)
    i = pl.program_id(2)
    tq = q_ref.shape[0]
    t = k_ref.shape[0]
    rows = r_heads * tq
    base = pl.multiple_of(jnp.clip(i * tq - WINDOW, 0, t - span), LANE)
    kw = k_ref[pl.ds(base, span), :]
    blk0 = base // LANE
    vwt = jnp.concatenate([vt_ref[blk0 + c] for c in range(span // LANE)], axis=1)
    kpos = base + lax.broadcasted_iota(jnp.int32, (span, rows), 0)
    qpos = i * tq + lax.broadcasted_iota(jnp.int32, (span, rows), 1) % tq
    valid = jnp.abs(qpos - kpos) <= WINDOW
    qs = _stack_heads(q_ref, r_heads)
    s_c = _dot_nt(kc_ref[...], qs)
    s_w = jnp.where(valid, _dot_nt(kw, qs), NEG_BIG)
    sk = jnp.concatenate(
        [jnp.full((1, tq), sink_ref[g * r_heads + r], F32) for r in range(r_heads)], axis=1)
    m = jnp.maximum(jnp.maximum(jnp.max(s_c, axis=0, keepdims=True),
                                jnp.max(s_w, axis=0, keepdims=True)), sk)
    e_c = jnp.exp(s_c - m)
    e_w = jnp.exp(s_w - m)
    den = (jnp.sum(e_c, axis=0, keepdims=True) + jnp.sum(e_w, axis=0, keepdims=True)
           + jnp.exp(sk - m))
    inv = 1.0 / den
    o_t = _dot(vct_ref[...], (e_c * inv).astype(BF16)) + _dot(vwt, (e_w * inv).astype(BF16))
    _store_heads_t(o_ref, o_t, r_heads)


def _window_attention(qkv, v_t, k_ctx, v_ctx_t, layer, sink, batch, q_dim, kv_dim):
    m = qkv.shape[0]
    t = m // batch
    n_kv = kv_dim // LANE
    r_heads = q_dim // kv_dim
    qw = r_heads * LANE
    tq = _tile(t, TILES["win_q"], LANE)
    span = tq + 2 * (-(-WINDOW // LANE) * LANE)
    assert t % tq == 0 and t >= span
    nb = t // tq
    p = k_ctx.shape[3]
    return pl.pallas_call(
        functools.partial(_win_attn_kernel, r_heads=r_heads, span=span),
        grid=(batch, n_kv, nb),
        in_specs=[
            pl.BlockSpec(memory_space=pltpu.SMEM),
            pl.BlockSpec((tq, qw), lambda b, g, i: (b * nb + i, g)),
            pl.BlockSpec((t, LANE), lambda b, g, i: (b, q_dim // LANE + g)),
            pl.BlockSpec((None, t // LANE, LANE, LANE), lambda b, g, i: (g, b, 0, 0)),
            pl.BlockSpec((None, None, None, p, LANE), lambda b, g, i: (b, layer, g, 0, 0)),
            pl.BlockSpec((None, None, None, LANE, p), lambda b, g, i: (b, layer, g, 0, 0)),
        ],
        out_specs=pl.BlockSpec((tq, qw), lambda b, g, i: (b * nb + i, g)),
        out_shape=jax.ShapeDtypeStruct((m, q_dim), BF16),
        compiler_params=_params("parallel", "parallel", "arbitrary"),
        name="window_attention",
    )(sink, qkv, qkv, v_t, k_ctx, v_ctx_t)


def _dense_attn_kernel(q_ref, k_ref, vt_ref, kc_ref, vct_ref, o_ref, kmax_ref, shift_ref, acc_ref, *,
                       r_heads, kv_chunk):
    t = k_ref.shape[0]

    @pl.when(pl.program_id(2) == 0)
    def _():
        def max_sq_norm(ref):
            x = ref[...].astype(F32)
            return jnp.max(jnp.sum(x * x, axis=1, keepdims=True))
        kmax_ref[0] = jnp.sqrt(jnp.maximum(max_sq_norm(k_ref), max_sq_norm(kc_ref)))

    qs = _stack_heads(q_ref, r_heads)
    rows = qs.shape[0]
    q_sq = _dot_nt(jnp.ones((SUB, LANE), BF16), qs * qs)[0:1]
    bound = jnp.sqrt(q_sq) * kmax_ref[0]

    shift_ref[...] = jnp.zeros_like(shift_ref)

    def attempt(carry):
        it, _ = carry
        shift = shift_ref[...]
        off = bound - shift
        l = jnp.zeros((1, rows), F32)
        acc = jnp.zeros((LANE, rows), F32)
        chunks = [(kc_ref[...], vct_ref[...])]
        chunks += [(k_ref[c * kv_chunk:(c + 1) * kv_chunk, :], vt_ref[:, c * kv_chunk:(c + 1) * kv_chunk])
                   for c in range(t // kv_chunk)]
        for kc, vct in chunks:
            p = jnp.exp2(_dot_nt(kc, qs) - off)
            l = l + jnp.sum(p, axis=0, keepdims=True)
            acc = acc + _dot(vct, p.astype(BF16))
        acc_ref[...] = acc * (1.0 / l)
        shift_ref[...] = jnp.where(l < UNDERFLOW_SUM, shift + RETRY_SHIFT, shift)
        return it + 1, jnp.min(l)

    def unfinished(carry):
        it, min_sum = carry
        return jnp.logical_or(it == 0, min_sum < UNDERFLOW_SUM)

    lax.while_loop(unfinished, attempt, (jnp.int32(0), jnp.float32(1.0)))
    _store_heads_t(o_ref, acc_ref[...], r_heads)


def _dense_attention(qkv, v_t, k_ctx, v_ctx_t, layer, batch, q_dim, kv_dim):
    m = qkv.shape[0]
    t = m // batch
    n_kv = kv_dim // LANE
    r_heads = q_dim // kv_dim
    qw = r_heads * LANE
    tq = _tile(t, TILES["attn_q"], LANE)
    kv_chunk = _tile(t, TILES["attn_kv"], LANE)
    nb = t // tq
    p = k_ctx.shape[3]
    return pl.pallas_call(
        functools.partial(_dense_attn_kernel, r_heads=r_heads, kv_chunk=kv_chunk),
        grid=(batch, n_kv, nb),
        in_specs=[
            pl.BlockSpec((tq, qw), lambda b, g, i: (b * nb + i, g)),
            pl.BlockSpec((t, LANE), lambda b, g, i: (b, q_dim // LANE + g)),
            pl.BlockSpec((LANE, t), lambda b, g, i: (g, b)),
            pl.BlockSpec((None, None, None, p, LANE), lambda b, g, i: (b, layer, g, 0, 0)),
            pl.BlockSpec((None, None, None, LANE, p), lambda b, g, i: (b, layer, g, 0, 0)),
        ],
        out_specs=pl.BlockSpec((tq, qw), lambda b, g, i: (b * nb + i, g)),
        out_shape=jax.ShapeDtypeStruct((m, q_dim), BF16),
        scratch_shapes=[pltpu.SMEM((1,), F32), pltpu.VMEM((1, r_heads * tq), F32),
                        pltpu.VMEM((LANE, r_heads * tq), F32)],
        compiler_params=_params("parallel", "parallel", "arbitrary"),
        name="dense_attention",
    )(qkv, qkv, v_t, k_ctx, v_ctx_t)


def _proj_res_kernel(a_ref, w_ref, x_ref, g_ref, o_ref):
    o_ref[...] = x_ref[...] + g_ref[...] * _dot(a_ref[...], w_ref[...])


def _proj_residual(a, w, layer, x, mod, gate_slot, bm_pref, bn_pref):
    m, k = a.shape
    d = w.shape[2]
    bm = _tile(min(m, mod.rows_per_group), bm_pref, 16)
    bn = _tile(d, bn_pref, LANE)
    nj = d // bn
    return pl.pallas_call(
        _proj_res_kernel,
        grid=(m // bm, nj),
        in_specs=[
            pl.BlockSpec((bm, k), lambda i, j: (i, 0)),
            pl.BlockSpec((None, k, bn), lambda i, j: (layer, 0, j)),
            pl.BlockSpec((bm, bn), lambda i, j: (i, j)),
            mod.spec(gate_slot, bm, bn, nj),
        ],
        out_specs=pl.BlockSpec((bm, bn), lambda i, j: (i, j)),
        out_shape=jax.ShapeDtypeStruct((m, d), F32),
        compiler_params=_params("parallel", "parallel"),
        name="proj_residual",
    )(a, w, x, mod.table)


def _matmul_kernel(a_ref, w_ref, o_ref):
    o_ref[...] = _dot(a_ref[...], w_ref[...])


def _edge_gate(h, wg, layer, bm):
    m, d = h.shape
    nt = m // bm
    fp = wg.shape[2]
    last = jnp.roll(h[bm - 1::bm], 1, axis=0)
    first = jnp.roll(h[0::bm], -1, axis=0)
    edge = jnp.zeros((nt, 2 * SUB, d), h.dtype).at[:, SUB - 1].set(last).at[:, SUB].set(first)
    bn = _tile(fp, TILES["ffn_f"], LANE)
    out = pl.pallas_call(
        _matmul_kernel,
        grid=(fp // bn,),
        in_specs=[pl.BlockSpec((nt * 2 * SUB, d), lambda j: (0, 0)),
                  pl.BlockSpec((None, d, bn), lambda j: (layer, 0, j))],
        out_specs=pl.BlockSpec((nt * 2 * SUB, bn), lambda j: (0, j)),
        out_shape=jax.ShapeDtypeStruct((nt * 2 * SUB, fp), F32),
        compiler_params=_params("parallel"),
        name="edge_gate",
    )(edge.reshape(nt * 2 * SUB, d), wg)
    return out.reshape(nt, 2 * SUB, fp)


def _gate_up_kernel(h_ref, wg_ref, wu_ref, cw_ref, cb_ref, e_ref, o_ref, a_ref, *, bm, seq_len):
    a_ref[0:SUB, :] = e_ref[0:SUB, :]
    a_ref[SUB:SUB + bm, :] = _dot(h_ref[...], wg_ref[...])
    a_ref[SUB + bm:, :] = e_ref[SUB:, :]
    u = _dot(h_ref[...], wu_ref[...])
    row = pl.program_id(0) * bm + lax.broadcasted_iota(jnp.int32, (bm, 1), 0)
    pos = row % seq_len
    prev = jnp.where(pos == 0, 0.0, a_ref[SUB - 1:SUB - 1 + bm, :])
    nxt = jnp.where(pos == seq_len - 1, 0.0, a_ref[SUB + 1:SUB + 1 + bm, :])
    cur = a_ref[SUB:SUB + bm, :]
    a = prev * cw_ref[0:1, :] + cur * cw_ref[1:2, :] + nxt * cw_ref[2:3, :] + cb_ref[...]
    o_ref[...] = (_silu(a) * u).astype(o_ref.dtype)


def _gate_up(h, wg, wu, conv_w, conv_b, layer, seq_len, bm):
    m, d = h.shape
    fp = wg.shape[2]
    bf = _tile(fp, TILES["ffn_f"], LANE)
    edge = _edge_gate(h, wg, layer, bm)
    return pl.pallas_call(
        functools.partial(_gate_up_kernel, bm=bm, seq_len=seq_len),
        grid=(m // bm, fp // bf),
        in_specs=[
            pl.BlockSpec((bm, d), lambda i, f: (i, 0)),
            pl.BlockSpec((None, d, bf), lambda i, f: (layer, 0, f)),
            pl.BlockSpec((None, d, bf), lambda i, f: (layer, 0, f)),
            pl.BlockSpec((None, conv_w.shape[1], bf), lambda i, f: (layer, 0, f)),
            pl.BlockSpec((None, 1, bf), lambda i, f: (layer, 0, f)),
            pl.BlockSpec((None, 2 * SUB, bf), lambda i, f: (i, 0, f)),
        ],
        out_specs=pl.BlockSpec((bm, bf), lambda i, f: (i, f)),
        out_shape=jax.ShapeDtypeStruct((m, fp), BF16),
        scratch_shapes=[pltpu.VMEM((bm + 2 * SUB, bf), F32)],
        compiler_params=_params("parallel", "parallel"),
        name="gate_up",
    )(h, wg, wu, conv_w, conv_b, edge)


def _rope_tables(n_tokens):
    axis_dim = LANE // 2
    half = axis_dim // 2
    rows = n_tokens // GRID_W
    row = jnp.repeat(jnp.arange(rows, dtype=jnp.int32), GRID_W)
    col = jnp.tile(jnp.arange(GRID_W, dtype=jnp.int32), rows)
    inv_freq = ROPE_THETA ** (-jnp.arange(half, dtype=F32) / half)
    cos_parts, sin_parts = [], []
    for pos in (row, col):
        ang = pos.astype(F32)[:, None] * inv_freq[None, :]
        c, s = jnp.cos(ang), jnp.sin(ang)
        cos_parts += [c, c]
        sin_parts += [-s, s]
    return jnp.concatenate(cos_parts, axis=1), jnp.concatenate(sin_parts, axis=1)


def _pad_last(a, n):
    return jnp.pad(a, [(0, 0)] * (a.ndim - 1) + [(0, n - a.shape[-1])])


def kernel(x_prompt, x_sample, cache_k, cache_v, c, c_ctx, w_mod, b_mod, norm_attn, norm_ffn, w_qkv, w_o,
           sink_a, q_norm_b, k_norm_b, w_gate, w_up, w_down, conv_w, conv_b, norm_f):
    batch_p, seq_p, d = x_prompt.shape
    batch_s, seq_s, _ = x_sample.shape
    depth = w_mod.shape[0]
    n_kv, head_dim = cache_k.shape[3], cache_k.shape[4]
    n_heads = sink_a.shape[1]
    assert head_dim == LANE
    q_dim, kv_dim = n_heads * head_dim, n_kv * head_dim
    d_ff = w_gate.shape[2]
    scale = head_dim ** -0.5
    m_p, m_s = batch_p * seq_p, batch_s * seq_s

    ffn_m_p = _tile(m_p, TILES["ffn_m"], seq_p)
    ffn_m_s = _tile(seq_s, TILES["ffn_m"], 16)
    bf = min(TILES["ffn_f"], -(-d_ff // LANE) * LANE)
    fp = -(-d_ff // bf) * bf

    xp = x_prompt.reshape(m_p, d)
    xs = x_sample.reshape(m_s, d)

    n_cond = -(-(batch_s + 1) // SUB) * SUB
    cond = jnp.zeros((n_cond, d), F32).at[:batch_s].set(c).at[batch_s].set(c_ctx)
    mod_table = _modulation(cond, w_mod, b_mod).reshape(depth * n_cond, 1, -1)

    wqkv = w_qkv.astype(BF16)
    wo = w_o.astype(BF16)
    wg = _pad_last(w_gate.astype(BF16), fp)
    wu = _pad_last(w_up.astype(BF16), fp)
    wd = jnp.pad(w_down.astype(BF16), ((0, 0), (0, fp - d_ff), (0, 0)))
    cw = _pad_last(conv_w, fp)
    cb = _pad_last(conv_b.reshape(depth, 1, d_ff), fp)
    gains_attn = norm_attn.reshape(depth, 1, d)
    gains_ffn = norm_ffn.reshape(depth, 1, d)
    k_ctx = cache_k.transpose(0, 1, 3, 2, 4).astype(BF16)
    v_ctx_t = cache_v.transpose(0, 1, 3, 4, 2).astype(BF16)

    rope_tables = _rope_tables(seq_s)
    ones_kv = jnp.ones((kv_dim,), F32)
    ks_out, vs_out = [], []
    for i in range(depth):
        mod_s = _Mod(mod_table, i * n_cond, seq_s)
        mod_p = _Mod(mod_table, i * n_cond + batch_s, m_p)
        mixer_a = i % 2 == 0
        if mixer_a:
            sink = sink_a[i // 2].astype(F32)
            colscale = jnp.concatenate([jnp.full((q_dim,), scale, F32), ones_kv, ones_kv])
        else:
            sink = jnp.zeros((n_heads,), F32)
            colscale = jnp.concatenate([jnp.tile(q_norm_b[i // 2].astype(F32), n_heads) * (scale * LOG2E),
                                        jnp.tile(k_norm_b[i // 2].astype(F32), n_kv), ones_kv])
        colscale = colscale.reshape(1, -1)

        hp = _norm_mod(xp, gains_attn, i, mod_p, 0)
        hs = _norm_mod(xs, gains_attn, i, mod_s, 0)
        qkv_p, kv32 = _qkv_proj(hp, wqkv, i, colscale, q_dim, kv_dim, norm=not mixer_a, rope_tables=None,
                                want_kv32=True)
        qkv_s, = _qkv_proj(hs, wqkv, i, colscale, q_dim, kv_dim, norm=not mixer_a, rope_tables=rope_tables,
                           want_kv32=False)
        ks_out.append(kv32[:, :kv_dim].reshape(batch_p, seq_p, n_kv, head_dim))
        vs_out.append(kv32[:, kv_dim:].reshape(batch_p, seq_p, n_kv, head_dim))
        op = _ctx_attention(qkv_p, sink, batch_p, q_dim, kv_dim, has_sink=mixer_a)
        v_s = qkv_s[:, q_dim + kv_dim:]
        if mixer_a:
            v_t = v_s.reshape(m_s // LANE, LANE, n_kv, LANE).transpose(2, 0, 3, 1)
            osm = _window_attention(qkv_s, v_t, k_ctx, v_ctx_t, i, sink, batch_s, q_dim, kv_dim)
        else:
            osm = _dense_attention(qkv_s, v_s.T, k_ctx, v_ctx_t, i, batch_s, q_dim, kv_dim)
        xp = _proj_residual(op, wo, i, xp, mod_p, 2, TILES["proj_m"], TILES["proj_n"])
        xs = _proj_residual(osm, wo, i, xs, mod_s, 2, TILES["proj_m"], TILES["proj_n"])

        hp = _norm_mod(xp, gains_ffn, i, mod_p, 3)
        hs = _norm_mod(xs, gains_ffn, i, mod_s, 3)
        zp = _gate_up(hp, wg, wu, cw, cb, i, seq_p, ffn_m_p)
        zs = _gate_up(hs, wg, wu, cw, cb, i, seq_s, ffn_m_s)
        xp = _proj_residual(zp, wd, i, xp, mod_p, 5, TILES["down_m"], TILES["down_n"])
        xs = _proj_residual(zs, wd, i, xs, mod_s, 5, TILES["down_m"], TILES["down_n"])

    y_prompt = _final_norm(xp, norm_f).reshape(batch_p, seq_p, d)
    y_sample = _final_norm(xs, norm_f).reshape(batch_s, seq_s, d)
    return y_prompt, y_sample, jnp.stack(ks_out, axis=1), jnp.stack(vs_out, axis=1)
```

```python
import functools
import math

import jax
import jax.numpy as jnp
from jax import lax
from jax.experimental import pallas as pl
from jax.experimental.pallas import tpu as pltpu

GRID_W = 64
WINDOW = 128
ROPE_THETA = 10000.0
EPS = 1e-6
LANE = 128
SUB = 8
NEG_BIG = -1e30
UNDERFLOW_SUM = 2.0 ** -60
RETRY_SHIFT = 64.0
LOG2E = math.log2(math.e)
VMEM_LIMIT_BYTES = 58 * 1024 * 1024

BF16 = jnp.bfloat16
F32 = jnp.float32

TILES = dict(
    mod_n=512,
    norm_m=256,
    qkv_m=1024, qkv_n=1024,
    proj_m=512, proj_n=1024,
    ffn_m=1024, ffn_f=512,
    down_m=512, down_n=512,
    win_q=256,
    attn_q=256, attn_kv=2048,
)


def _tile(dim, pref, mult):
    if dim <= pref:
        return dim
    t = pref - pref % mult
    while t >= mult:
        if dim % t == 0:
            return t
        t -= mult
    return dim


def _params(*semantics):
    return pltpu.CompilerParams(dimension_semantics=semantics, vmem_limit_bytes=VMEM_LIMIT_BYTES)


def _silu(x):
    return x / (1.0 + jnp.exp(-x))


def _dot(a, b):
    return jnp.dot(a, b, preferred_element_type=F32)


def _dot_nt(a, b):
    return lax.dot_general(a, b, (((1,), (1,)), ((), ())), preferred_element_type=F32)


def _mod_kernel(c_ref, w_ref, b_ref, o_ref):
    a = _silu(c_ref[...]).astype(BF16)
    o_ref[...] = _dot(a, w_ref[...].astype(BF16)) + b_ref[...]


def _modulation(cond, w_mod, b_mod):
    depth, d, n = w_mod.shape
    rows = cond.shape[0]
    tn = _tile(n, TILES["mod_n"], LANE)
    return pl.pallas_call(
        _mod_kernel,
        grid=(depth, n // tn),
        in_specs=[
            pl.BlockSpec((rows, d), lambda l, j: (0, 0)),
            pl.BlockSpec((None, d, tn), lambda l, j: (l, 0, j)),
            pl.BlockSpec((None, 1, tn), lambda l, j: (l, 0, j)),
        ],
        out_specs=pl.BlockSpec((None, rows, tn), lambda l, j: (l, 0, j)),
        out_shape=jax.ShapeDtypeStruct((depth, rows, n), F32),
        compiler_params=_params("parallel", "parallel"),
        name="modulation",
    )(cond, w_mod, b_mod.reshape(depth, 1, n))


class _Mod:
    def __init__(self, table, base, rows_per_group):
        self.table, self.base, self.rows_per_group = table, base, rows_per_group

    def spec(self, slot, bm, bn, nj):
        base, rpg = self.base, self.rows_per_group
        if nj == 1:
            return pl.BlockSpec((None, 1, bn), lambda i, *_: (base + (i * bm) // rpg, 0, slot))
        return pl.BlockSpec((None, 1, bn), lambda i, j: (base + (i * bm) // rpg, 0, slot * nj + j))


def _normed(x, gain):
    return x * lax.rsqrt(jnp.mean(x * x, axis=-1, keepdims=True) + EPS) * gain


def _norm_mod_kernel(x_ref, g_ref, sh_ref, sc_ref, o_ref):
    x = x_ref[...]
    inv = lax.rsqrt(jnp.mean(x * x, axis=-1, keepdims=True) + EPS)
    o_ref[...] = (x_ref[...] * inv * g_ref[...] * (1.0 + sc_ref[...]) + sh_ref[...]).astype(o_ref.dtype)


def _norm_mod(x, gains, layer, mod, shift_slot):
    m, d = x.shape
    bm = _tile(min(m, mod.rows_per_group), TILES["norm_m"], SUB)
    return pl.pallas_call(
        _norm_mod_kernel,
        grid=(m // bm,),
        in_specs=[
            pl.BlockSpec((bm, d), lambda i: (i, 0)),
            pl.BlockSpec((None, 1, d), lambda i: (layer, 0, 0)),
            mod.spec(shift_slot, bm, d, 1),
            mod.spec(shift_slot + 1, bm, d, 1),
        ],
        out_specs=pl.BlockSpec((bm, d), lambda i: (i, 0)),
        out_shape=jax.ShapeDtypeStruct((m, d), BF16),
        compiler_params=_params("parallel"),
        name="norm_mod",
    )(x, gains, mod.table, mod.table)


def _norm_kernel(x_ref, g_ref, o_ref):
    o_ref[...] = _normed(x_ref[...], g_ref[...])


def _final_norm(x, gain):
    m, d = x.shape
    bm = _tile(m, TILES["norm_m"], SUB)
    return pl.pallas_call(
        _norm_kernel,
        grid=(m // bm,),
        in_specs=[pl.BlockSpec((bm, d), lambda i: (i, 0)), pl.BlockSpec((1, d), lambda i: (0, 0))],
        out_specs=pl.BlockSpec((bm, d), lambda i: (i, 0)),
        out_shape=jax.ShapeDtypeStruct((m, d), F32),
        compiler_params=_params("parallel"),
        name="final_norm",
    )(x, gain.reshape(1, d))


def _qkv_kernel(*refs, n_q, n_k, norm, rope, want_kv32):
    h_ref, w_ref, cs_ref = refs[:3]
    refs = refs[3:]
    if rope:
        cos_ref, sin_ref = refs[:2]
        refs = refs[2:]
    o_ref = refs[0]
    kv_ref = refs[1] if want_kv32 else None
    is_v = pl.program_id(1) >= n_q + n_k
    acc = _dot(h_ref[...], w_ref[...])
    bn = acc.shape[1]
    if rope:
        cos = jnp.where(is_v, 1.0, cos_ref[...])
        sin = jnp.where(is_v, 0.0, sin_ref[...])
        lane = lax.broadcasted_iota(jnp.int32, (1, LANE), 1)
        first_half = (lane % (LANE // 2)) < (LANE // 4)
    outs = []
    for hh in range(bn // LANE):
        blk = acc[:, hh * LANE:(hh + 1) * LANE]
        if norm:
            inv = lax.rsqrt(jnp.mean(blk * blk, axis=-1, keepdims=True) + EPS)
            blk = blk * jnp.where(is_v, 1.0, inv)
        blk = blk * cs_ref[:, hh * LANE:(hh + 1) * LANE]
        if rope:
            partner = jnp.where(first_half, pltpu.roll(blk, LANE - LANE // 4, 1),
                                pltpu.roll(blk, LANE // 4, 1))
            blk = blk * cos + partner * sin
        outs.append(blk)
    res = jnp.concatenate(outs, axis=1) if len(outs) > 1 else outs[0]
    o_ref[...] = res.astype(o_ref.dtype)
    if want_kv32:
        kv_ref[...] = res


def _qkv_proj(h, w, layer, colscale, q_dim, kv_dim, *, norm, rope_tables, want_kv32):
    m, d = h.shape
    n = w.shape[2]
    bm = _tile(m, TILES["qkv_m"], 16)
    bn = _tile(kv_dim, TILES["qkv_n"], LANE)
    assert q_dim % bn == 0 and kv_dim % bn == 0
    n_q, n_k = q_dim // bn, kv_dim // bn
    rope = rope_tables is not None
    in_specs = [
        pl.BlockSpec((bm, d), lambda i, j: (i, 0)),
        pl.BlockSpec((None, d, bn), lambda i, j: (layer, 0, j)),
        pl.BlockSpec((1, bn), lambda i, j: (0, j)),
    ]
    args = [h, w, colscale]
    if rope:
        cos, sin = rope_tables
        t = cos.shape[0]
        assert t % bm == 0
        per = t // bm
        in_specs += [pl.BlockSpec((bm, LANE), lambda i, j: (i % per, 0))] * 2
        args += [cos, sin]
    out_specs = [pl.BlockSpec((bm, bn), lambda i, j: (i, j))]
    out_shape = [jax.ShapeDtypeStruct((m, n), BF16)]
    if want_kv32:
        out_specs.append(pl.BlockSpec((bm, bn), lambda i, j: (i, jnp.maximum(j - n_q, 0))))
        out_shape.append(jax.ShapeDtypeStruct((m, 2 * kv_dim), F32))
    return pl.pallas_call(
        functools.partial(_qkv_kernel, n_q=n_q, n_k=n_k, norm=norm, rope=rope, want_kv32=want_kv32),
        grid=(m // bm, n // bn),
        in_specs=in_specs,
        out_specs=out_specs,
        out_shape=out_shape,
        compiler_params=_params("parallel", "arbitrary"),
        name="qkv_proj",
    )(*args)


def _stack_heads(q_ref, r_heads):
    return jnp.concatenate([q_ref[:, r * LANE:(r + 1) * LANE] for r in range(r_heads)], axis=0)


def _ctx_attn_kernel(sink_ref, q_ref, k_ref, v_ref, o_ref, *, r_heads, has_sink):
    g = pl.program_id(1)
    tq = q_ref.shape[0]
    ex = jnp.exp if has_sink else jnp.exp2
    qs = _stack_heads(q_ref, r_heads)
    s = _dot_nt(qs, k_ref[...])
    m = jnp.max(s, axis=-1, keepdims=True)
    if has_sink:
        sk = jnp.concatenate(
            [jnp.full((tq, 1), sink_ref[g * r_heads + r], F32) for r in range(r_heads)], axis=0)
        m = jnp.maximum(m, sk)
    e = ex(s - m)
    den = jnp.sum(e, axis=-1, keepdims=True)
    if has_sink:
        den = den + ex(sk - m)
    p = (e * (1.0 / den)).astype(BF16)
    o = _dot(p, v_ref[...])
    for r in range(r_heads):
        o_ref[:, r * LANE:(r + 1) * LANE] = o[r * tq:(r + 1) * tq].astype(o_ref.dtype)


def _ctx_attention(qkv, sink, batch, q_dim, kv_dim, *, has_sink):
    m = qkv.shape[0]
    t = m // batch
    n_kv = kv_dim // LANE
    r_heads = q_dim // kv_dim
    qw = r_heads * LANE
    return pl.pallas_call(
        functools.partial(_ctx_attn_kernel, r_heads=r_heads, has_sink=has_sink),
        grid=(batch, n_kv),
        in_specs=[
            pl.BlockSpec(memory_space=pltpu.SMEM),
            pl.BlockSpec((t, qw), lambda b, g: (b, g)),
            pl.BlockSpec((t, LANE), lambda b, g: (b, q_dim // LANE + g)),
            pl.BlockSpec((t, LANE), lambda b, g: (b, (q_dim + kv_dim) // LANE + g)),
        ],
        out_specs=pl.BlockSpec((t, qw), lambda b, g: (b, g)),
        out_shape=jax.ShapeDtypeStruct((m, q_dim), BF16),
        compiler_params=_params("parallel", "parallel"),
        name="ctx_attention",
    )(sink, qkv, qkv, qkv)


def _store_heads_t(o_ref, o_t, r_heads):
    tq = o_t.shape[1] // r_heads
    for r in range(r_heads):
        o_ref[:, r * LANE:(r + 1) * LANE] = o_t[:, r * tq:(r + 1) * tq].T.astype(o_ref.dtype)


def _win_attn_kernel(sink_ref, q_ref, k_ref, vt_ref, kc_ref, vct_ref, o_ref, *, r_heads, span):
    g = pl.program_id(1)
    i = pl.program_id(2)
    tq = q_ref.shape[0]
    t = k_ref.shape[0]
    rows = r_heads * tq
    base = pl.multiple_of(jnp.clip(i * tq - WINDOW, 0, t - span), LANE)
    kw = k_ref[pl.ds(base, span), :]
    blk0 = base // LANE
    vwt = jnp.concatenate([vt_ref[blk0 + c] for c in range(span // LANE)], axis=1)
    kpos = base + lax.broadcasted_iota(jnp.int32, (span, rows), 0)
    qpos = i * tq + lax.broadcasted_iota(jnp.int32, (span, rows), 1) % tq
    valid = jnp.abs(qpos - kpos) <= WINDOW
    qs = _stack_heads(q_ref, r_heads)
    s_c = _dot_nt(kc_ref[...], qs)
    s_w = jnp.where(valid, _dot_nt(kw, qs), NEG_BIG)
    sk = jnp.concatenate(
        [jnp.full((1, tq), sink_ref[g * r_heads + r], F32) for r in range(r_heads)], axis=1)
    m = jnp.maximum(jnp.maximum(jnp.max(s_c, axis=0, keepdims=True),
                                jnp.max(s_w, axis=0, keepdims=True)), sk)
    e_c = jnp.exp(s_c - m)
    e_w = jnp.exp(s_w - m)
    den = (jnp.sum(e_c, axis=0, keepdims=True) + jnp.sum(e_w, axis=0, keepdims=True)
           + jnp.exp(sk - m))
    inv = 1.0 / den
    o_t = _dot(vct_ref[...], (e_c * inv).astype(BF16)) + _dot(vwt, (e_w * inv).astype(BF16))
    _store_heads_t(o_ref, o_t, r_heads)


def _window_attention(qkv, v_t, k_ctx, v_ctx_t, layer, sink, batch, q_dim, kv_dim):
    m = qkv.shape[0]
    t = m // batch
    n_kv = kv_dim // LANE
    r_heads = q_dim // kv_dim
    qw = r_heads * LANE
    tq = _tile(t, TILES["win_q"], LANE)
    span = tq + 2 * (-(-WINDOW // LANE) * LANE)
    assert t % tq == 0 and t >= span
    nb = t // tq
    p = k_ctx.shape[3]
    return pl.pallas_call(
        functools.partial(_win_attn_kernel, r_heads=r_heads, span=span),
        grid=(batch, n_kv, nb),
        in_specs=[
            pl.BlockSpec(memory_space=pltpu.SMEM),
            pl.BlockSpec((tq, qw), lambda b, g, i: (b * nb + i, g)),
            pl.BlockSpec((t, LANE), lambda b, g, i: (b, q_dim // LANE + g)),
            pl.BlockSpec((None, t // LANE, LANE, LANE), lambda b, g, i: (g, b, 0, 0)),
            pl.BlockSpec((None, None, None, p, LANE), lambda b, g, i: (b, layer, g, 0, 0)),
            pl.BlockSpec((None, None, None, LANE, p), lambda b, g, i: (b, layer, g, 0, 0)),
        ],
        out_specs=pl.BlockSpec((tq, qw), lambda b, g, i: (b * nb + i, g)),
        out_shape=jax.ShapeDtypeStruct((m, q_dim), BF16),
        compiler_params=_params("parallel", "parallel", "arbitrary"),
        name="window_attention",
    )(sink, qkv, qkv, v_t, k_ctx, v_ctx_t)


def _dense_attn_kernel(q_ref, k_ref, vt_ref, kc_ref, vct_ref, o_ref, kmax_ref, shift_ref, acc_ref, *,
                       r_heads, kv_chunk):
    t = k_ref.shape[0]

    @pl.when(pl.program_id(2) == 0)
    def _():
        def max_sq_norm(ref):
            x = ref[...].astype(F32)
            return jnp.max(jnp.sum(x * x, axis=1, keepdims=True))
        kmax_ref[0] = jnp.sqrt(jnp.maximum(max_sq_norm(k_ref), max_sq_norm(kc_ref)))

    qs = _stack_heads(q_ref, r_heads)
    rows = qs.shape[0]
    q_sq = _dot_nt(jnp.ones((SUB, LANE), BF16), qs * qs)[0:1]
    bound = jnp.sqrt(q_sq) * kmax_ref[0]

    shift_ref[...] = jnp.zeros_like(shift_ref)

    def attempt(carry):
        it, _ = carry
        shift = shift_ref[...]
        off = bound - shift
        l = jnp.zeros((1, rows), F32)
        acc = jnp.zeros((LANE, rows), F32)
        chunks = [(kc_ref[...], vct_ref[...])]
        chunks += [(k_ref[c * kv_chunk:(c + 1) * kv_chunk, :], vt_ref[:, c * kv_chunk:(c + 1) * kv_chunk])
                   for c in range(t // kv_chunk)]
        for kc, vct in chunks:
            p = jnp.exp2(_dot_nt(kc, qs) - off)
            l = l + jnp.sum(p, axis=0, keepdims=True)
            acc = acc + _dot(vct, p.astype(BF16))
        acc_ref[...] = acc * (1.0 / l)
        shift_ref[...] = jnp.where(l < UNDERFLOW_SUM, shift + RETRY_SHIFT, shift)
        return it + 1, jnp.min(l)

    def unfinished(carry):
        it, min_sum = carry
        return jnp.logical_or(it == 0, min_sum < UNDERFLOW_SUM)

    lax.while_loop(unfinished, attempt, (jnp.int32(0), jnp.float32(1.0)))
    _store_heads_t(o_ref, acc_ref[...], r_heads)


def _dense_attention(qkv, v_t, k_ctx, v_ctx_t, layer, batch, q_dim, kv_dim):
    m = qkv.shape[0]
    t = m // batch
    n_kv = kv_dim // LANE
    r_heads = q_dim // kv_dim
    qw = r_heads * LANE
    tq = _tile(t, TILES["attn_q"], LANE)
    kv_chunk = _tile(t, TILES["attn_kv"], LANE)
    nb = t // tq
    p = k_ctx.shape[3]
    return pl.pallas_call(
        functools.partial(_dense_attn_kernel, r_heads=r_heads, kv_chunk=kv_chunk),
        grid=(batch, n_kv, nb),
        in_specs=[
            pl.BlockSpec((tq, qw), lambda b, g, i: (b * nb + i, g)),
            pl.BlockSpec((t, LANE), lambda b, g, i: (b, q_dim // LANE + g)),
            pl.BlockSpec((LANE, t), lambda b, g, i: (g, b)),
            pl.BlockSpec((None, None, None, p, LANE), lambda b, g, i: (b, layer, g, 0, 0)),
            pl.BlockSpec((None, None, None, LANE, p), lambda b, g, i: (b, layer, g, 0, 0)),
        ],
        out_specs=pl.BlockSpec((tq, qw), lambda b, g, i: (b * nb + i, g)),
        out_shape=jax.ShapeDtypeStruct((m, q_dim), BF16),
        scratch_shapes=[pltpu.SMEM((1,), F32), pltpu.VMEM((1, r_heads * tq), F32),
                        pltpu.VMEM((LANE, r_heads * tq), F32)],
        compiler_params=_params("parallel", "parallel", "arbitrary"),
        name="dense_attention",
    )(qkv, qkv, v_t, k_ctx, v_ctx_t)


def _proj_res_kernel(a_ref, w_ref, x_ref, g_ref, o_ref):
    o_ref[...] = x_ref[...] + g_ref[...] * _dot(a_ref[...], w_ref[...])


def _proj_residual(a, w, layer, x, mod, gate_slot, bm_pref, bn_pref):
    m, k = a.shape
    d = w.shape[2]
    bm = _tile(min(m, mod.rows_per_group), bm_pref, 16)
    bn = _tile(d, bn_pref, LANE)
    nj = d // bn
    return pl.pallas_call(
        _proj_res_kernel,
        grid=(m // bm, nj),
        in_specs=[
            pl.BlockSpec((bm, k), lambda i, j: (i, 0)),
            pl.BlockSpec((None, k, bn), lambda i, j: (layer, 0, j)),
            pl.BlockSpec((bm, bn), lambda i, j: (i, j)),
            mod.spec(gate_slot, bm, bn, nj),
        ],
        out_specs=pl.BlockSpec((bm, bn), lambda i, j: (i, j)),
        out_shape=jax.ShapeDtypeStruct((m, d), F32),
        compiler_params=_params("parallel", "parallel"),
        name="proj_residual",
    )(a, w, x, mod.table)


def _matmul_kernel(a_ref, w_ref, o_ref):
    o_ref[...] = _dot(a_ref[...], w_ref[...])


def _edge_gate(h, wg, layer, bm):
    m, d = h.shape
    nt = m // bm
    d_ff = wg.shape[2]
    last = jnp.roll(h[bm - 1::bm], 1, axis=0)
    first = jnp.roll(h[0::bm], -1, axis=0)
    edge = jnp.zeros((nt, 2 * SUB, d), h.dtype).at[:, SUB - 1].set(last).at[:, SUB].set(first)
    bn = min(d_ff, TILES["ffn_f"])
    out = pl.pallas_call(
        _matmul_kernel,
        grid=(pl.cdiv(d_ff, bn),),
        in_specs=[pl.BlockSpec((nt * 2 * SUB, d), lambda j: (0, 0)),
                  pl.BlockSpec((None, d, bn), lambda j: (layer, 0, j))],
        out_specs=pl.BlockSpec((nt * 2 * SUB, bn), lambda j: (0, j)),
        out_shape=jax.ShapeDtypeStruct((nt * 2 * SUB, d_ff), F32),
        compiler_params=_params("parallel"),
        name="edge_gate",
    )(edge.reshape(nt * 2 * SUB, d), wg)
    return out.reshape(nt, 2 * SUB, d_ff)


def _gate_up_kernel(h_ref, wg_ref, wu_ref, cw_ref, cb_ref, e_ref, o_ref, a_ref, *, bm, seq_len):
    a_ref[0:SUB, :] = e_ref[0:SUB, :]
    a_ref[SUB:SUB + bm, :] = _dot(h_ref[...], wg_ref[...])
    a_ref[SUB + bm:, :] = e_ref[SUB:, :]
    u = _dot(h_ref[...], wu_ref[...])
    row = pl.program_id(0) * bm + lax.broadcasted_iota(jnp.int32, (bm, 1), 0)
    pos = row % seq_len
    prev = jnp.where(pos == 0, 0.0, a_ref[SUB - 1:SUB - 1 + bm, :])
    nxt = jnp.where(pos == seq_len - 1, 0.0, a_ref[SUB + 1:SUB + 1 + bm, :])
    cur = a_ref[SUB:SUB + bm, :]
    a = prev * cw_ref[0:1, :] + cur * cw_ref[1:2, :] + nxt * cw_ref[2:3, :] + cb_ref[...]
    o_ref[...] = (_silu(a) * u).astype(o_ref.dtype)


def _gate_up(h, wg, wu, conv_w, conv_b, layer, seq_len, bm):
    m, d = h.shape
    d_ff = wg.shape[2]
    bf = min(d_ff, TILES["ffn_f"])
    edge = _edge_gate(h, wg, layer, bm)
    return pl.pallas_call(
        functools.partial(_gate_up_kernel, bm=bm, seq_len=seq_len),
        grid=(m // bm, pl.cdiv(d_ff, bf)),
        in_specs=[
            pl.BlockSpec((bm, d), lambda i, f: (i, 0)),
            pl.BlockSpec((None, d, bf), lambda i, f: (layer, 0, f)),
            pl.BlockSpec((None, d, bf), lambda i, f: (layer, 0, f)),
            pl.BlockSpec((None, conv_w.shape[1], bf), lambda i, f: (layer, 0, f)),
            pl.BlockSpec((None, 1, bf), lambda i, f: (layer, 0, f)),
            pl.BlockSpec((None, 2 * SUB, bf), lambda i, f: (i, 0, f)),
        ],
        out_specs=pl.BlockSpec((bm, bf), lambda i, f: (i, f)),
        out_shape=jax.ShapeDtypeStruct((m, d_ff), BF16),
        scratch_shapes=[pltpu.VMEM((bm + 2 * SUB, bf), F32)],
        compiler_params=_params("parallel", "parallel"),
        name="gate_up",
    )(h, wg, wu, conv_w, conv_b, edge)


def _rope_tables(n_tokens):
    axis_dim = LANE // 2
    half = axis_dim // 2
    rows = n_tokens // GRID_W
    row = jnp.repeat(jnp.arange(rows, dtype=jnp.int32), GRID_W)
    col = jnp.tile(jnp.arange(GRID_W, dtype=jnp.int32), rows)
    inv_freq = ROPE_THETA ** (-jnp.arange(half, dtype=F32) / half)
    cos_parts, sin_parts = [], []
    for pos in (row, col):
        ang = pos.astype(F32)[:, None] * inv_freq[None, :]
        c, s = jnp.cos(ang), jnp.sin(ang)
        cos_parts += [c, c]
        sin_parts += [-s, s]
    return jnp.concatenate(cos_parts, axis=1), jnp.concatenate(sin_parts, axis=1)


def kernel(x_prompt, x_sample, cache_k, cache_v, c, c_ctx, w_mod, b_mod, norm_attn, norm_ffn, w_qkv, w_o,
           sink_a, q_norm_b, k_norm_b, w_gate, w_up, w_down, conv_w, conv_b, norm_f):
    batch_p, seq_p, d = x_prompt.shape
    batch_s, seq_s, _ = x_sample.shape
    depth = w_mod.shape[0]
    n_kv, head_dim = cache_k.shape[3], cache_k.shape[4]
    n_heads = sink_a.shape[1]
    assert head_dim == LANE
    q_dim, kv_dim = n_heads * head_dim, n_kv * head_dim
    d_ff = w_gate.shape[2]
    scale = head_dim ** -0.5
    m_p, m_s = batch_p * seq_p, batch_s * seq_s

    ffn_m_p = _tile(m_p, TILES["ffn_m"], seq_p)
    ffn_m_s = _tile(seq_s, TILES["ffn_m"], 16)

    xp = x_prompt.reshape(m_p, d)
    xs = x_sample.reshape(m_s, d)

    n_cond = -(-(batch_s + 1) // SUB) * SUB
    cond = jnp.zeros((n_cond, d), F32).at[:batch_s].set(c).at[batch_s].set(c_ctx)
    mod_table = _modulation(cond, w_mod, b_mod).reshape(depth * n_cond, 1, -1)

    wqkv = w_qkv.astype(BF16)
    wo = w_o.astype(BF16)
    wg = w_gate.astype(BF16)
    wu = w_up.astype(BF16)
    wd = w_down.astype(BF16)
    cw = conv_w
    cb = conv_b.reshape(depth, 1, d_ff)
    gains_attn = norm_attn.reshape(depth, 1, d)
    gains_ffn = norm_ffn.reshape(depth, 1, d)
    k_ctx = cache_k.transpose(0, 1, 3, 2, 4).astype(BF16)
    v_ctx_t = cache_v.transpose(0, 1, 3, 4, 2).astype(BF16)

    rope_tables = _rope_tables(seq_s)
    ones_kv = jnp.ones((kv_dim,), F32)
    ks_out, vs_out = [], []
    for i in range(depth):
        mod_s = _Mod(mod_table, i * n_cond, seq_s)
        mod_p = _Mod(mod_table, i * n_cond + batch_s, m_p)
        mixer_a = i % 2 == 0
        if mixer_a:
            sink = sink_a[i // 2].astype(F32)
            colscale = jnp.concatenate([jnp.full((q_dim,), scale, F32), ones_kv, ones_kv])
        else:
            sink = jnp.zeros((n_heads,), F32)
            colscale = jnp.concatenate([jnp.tile(q_norm_b[i // 2].astype(F32), n_heads) * (scale * LOG2E),
                                        jnp.tile(k_norm_b[i // 2].astype(F32), n_kv), ones_kv])
        colscale = colscale.reshape(1, -1)

        hp = _norm_mod(xp, gains_attn, i, mod_p, 0)
        hs = _norm_mod(xs, gains_attn, i, mod_s, 0)
        qkv_p, kv32 = _qkv_proj(hp, wqkv, i, colscale, q_dim, kv_dim, norm=not mixer_a, rope_tables=None,
                                want_kv32=True)
        qkv_s, = _qkv_proj(hs, wqkv, i, colscale, q_dim, kv_dim, norm=not mixer_a, rope_tables=rope_tables,
                           want_kv32=False)
        ks_out.append(kv32[:, :kv_dim].reshape(batch_p, seq_p, n_kv, head_dim))
        vs_out.append(kv32[:, kv_dim:].reshape(batch_p, seq_p, n_kv, head_dim))
        op = _ctx_attention(qkv_p, sink, batch_p, q_dim, kv_dim, has_sink=mixer_a)
        v_s = qkv_s[:, q_dim + kv_dim:]
        if mixer_a:
            v_t = v_s.reshape(m_s // LANE, LANE, n_kv, LANE).transpose(2, 0, 3, 1)
            osm = _window_attention(qkv_s, v_t, k_ctx, v_ctx_t, i, sink, batch_s, q_dim, kv_dim)
        else:
            osm = _dense_attention(qkv_s, v_s.T, k_ctx, v_ctx_t, i, batch_s, q_dim, kv_dim)
        xp = _proj_residual(op, wo, i, xp, mod_p, 2, TILES["proj_m"], TILES["proj_n"])
        xs = _proj_residual(osm, wo, i, xs, mod_s, 2, TILES["proj_m"], TILES["proj_n"])

        hp = _norm_mod(xp, gains_ffn, i, mod_p, 3)
        hs = _norm_mod(xs, gains_ffn, i, mod_s, 3)
        zp = _gate_up(hp, wg, wu, cw, cb, i, seq_p, ffn_m_p)
        zs = _gate_up(hs, wg, wu, cw, cb, i, seq_s, ffn_m_s)
        xp = _proj_residual(zp, wd, i, xp, mod_p, 5, TILES["down_m"], TILES["down_n"])
        xs = _proj_residual(zs, wd, i, xs, mod_s, 5, TILES["down_m"], TILES["down_n"])

    y_prompt = _final_norm(xp, norm_f).reshape(batch_p, seq_p, d)
    y_sample = _final_norm(xs, norm_f).reshape(batch_s, seq_s, d)
    return y_prompt, y_sample, jnp.stack(ks_out, axis=1), jnp.stack(vs_out, axis=1)
```

```python
import functools
import math

import jax
import jax.numpy as jnp
from jax import lax
from jax.experimental import pallas as pl
from jax.experimental.pallas import tpu as pltpu

GRID_W = 64
WINDOW = 128
ROPE_THETA = 10000.0
EPS = 1e-6
LANE = 128
SUB = 8
NEG_BIG = -1e30
UNDERFLOW_SUM = 2.0 ** -60
RETRY_SHIFT = 64.0
LOG2E = math.log2(math.e)
VMEM_LIMIT_BYTES = 58 * 1024 * 1024

BF16 = jnp.bfloat16
F32 = jnp.float32

TILES = dict(
    mod_n=512,
    norm_m=256,
    qkv_m=1024, qkv_n=1024,
    proj_m=1024, proj_n=1024,
    ffn_m=1024, ffn_f=512,
    down_m=512, down_n=512,
    win_q=256,
    attn_q=256, attn_kv=2048,
)


def _tile(dim, pref, mult):
    if dim <= pref:
        return dim
    t = pref - pref % mult
    while t >= mult:
        if dim % t == 0:
            return t
        t -= mult
    return dim


def _params(*semantics):
    return pltpu.CompilerParams(dimension_semantics=semantics, vmem_limit_bytes=VMEM_LIMIT_BYTES)


def _silu(x):
    return x / (1.0 + jnp.exp(-x))


def _dot(a, b):
    return jnp.dot(a, b, preferred_element_type=F32)


def _dot_nt(a, b):
    return lax.dot_general(a, b, (((1,), (1,)), ((), ())), preferred_element_type=F32)


def _mod_kernel(c_ref, w_ref, b_ref, o_ref):
    a = _silu(c_ref[...]).astype(BF16)
    o_ref[...] = _dot(a, w_ref[...].astype(BF16)) + b_ref[...]


def _modulation(cond, w_mod, b_mod):
    depth, d, n = w_mod.shape
    rows = cond.shape[0]
    tn = _tile(n, TILES["mod_n"], LANE)
    return pl.pallas_call(
        _mod_kernel,
        grid=(depth, n // tn),
        in_specs=[
            pl.BlockSpec((rows, d), lambda l, j: (0, 0)),
            pl.BlockSpec((None, d, tn), lambda l, j: (l, 0, j)),
            pl.BlockSpec((None, 1, tn), lambda l, j: (l, 0, j)),
        ],
        out_specs=pl.BlockSpec((None, rows, tn), lambda l, j: (l, 0, j)),
        out_shape=jax.ShapeDtypeStruct((depth, rows, n), F32),
        compiler_params=_params("parallel", "parallel"),
        name="modulation",
    )(cond, w_mod, b_mod.reshape(depth, 1, n))


class _Mod:
    def __init__(self, table, base, rows_per_group):
        self.table, self.base, self.rows_per_group = table, base, rows_per_group

    def spec(self, slot, bm, bn, nj):
        base, rpg = self.base, self.rows_per_group
        if nj == 1:
            return pl.BlockSpec((None, 1, bn), lambda i, *_: (base + (i * bm) // rpg, 0, slot))
        return pl.BlockSpec((None, 1, bn), lambda i, j: (base + (i * bm) // rpg, 0, slot * nj + j))


def _normed(x, gain):
    return x * lax.rsqrt(jnp.mean(x * x, axis=-1, keepdims=True) + EPS) * gain


def _norm_mod_kernel(x_ref, g_ref, sh_ref, sc_ref, o_ref):
    x = x_ref[...]
    inv = lax.rsqrt(jnp.mean(x * x, axis=-1, keepdims=True) + EPS)
    o_ref[...] = (x_ref[...] * inv * g_ref[...] * (1.0 + sc_ref[...]) + sh_ref[...]).astype(o_ref.dtype)


def _norm_mod(x, gains, layer, mod, shift_slot):
    m, d = x.shape
    bm = _tile(min(m, mod.rows_per_group), TILES["norm_m"], SUB)
    return pl.pallas_call(
        _norm_mod_kernel,
        grid=(m // bm,),
        in_specs=[
            pl.BlockSpec((bm, d), lambda i: (i, 0)),
            pl.BlockSpec((None, 1, d), lambda i: (layer, 0, 0)),
            mod.spec(shift_slot, bm, d, 1),
            mod.spec(shift_slot + 1, bm, d, 1),
        ],
        out_specs=pl.BlockSpec((bm, d), lambda i: (i, 0)),
        out_shape=jax.ShapeDtypeStruct((m, d), BF16),
        compiler_params=_params("parallel"),
        name="norm_mod",
    )(x, gains, mod.table, mod.table)


def _norm_kernel(x_ref, g_ref, o_ref):
    o_ref[...] = _normed(x_ref[...], g_ref[...])


def _final_norm(x, gain):
    m, d = x.shape
    bm = _tile(m, TILES["norm_m"], SUB)
    return pl.pallas_call(
        _norm_kernel,
        grid=(m // bm,),
        in_specs=[pl.BlockSpec((bm, d), lambda i: (i, 0)), pl.BlockSpec((1, d), lambda i: (0, 0))],
        out_specs=pl.BlockSpec((bm, d), lambda i: (i, 0)),
        out_shape=jax.ShapeDtypeStruct((m, d), F32),
        compiler_params=_params("parallel"),
        name="final_norm",
    )(x, gain.reshape(1, d))


def _qkv_kernel(*refs, n_q, n_k, norm, rope, want_kv32):
    h_ref, w_ref, cs_ref = refs[:3]
    refs = refs[3:]
    if rope:
        cos_ref, sin_ref = refs[:2]
        refs = refs[2:]
    o_ref = refs[0]
    kv_ref = refs[1] if want_kv32 else None
    is_v = pl.program_id(1) >= n_q + n_k
    acc = _dot(h_ref[...], w_ref[...])
    bn = acc.shape[1]
    if rope:
        cos = jnp.where(is_v, 1.0, cos_ref[...])
        sin = jnp.where(is_v, 0.0, sin_ref[...])
        lane = lax.broadcasted_iota(jnp.int32, (1, LANE), 1)
        first_half = (lane % (LANE // 2)) < (LANE // 4)
    outs = []
    for hh in range(bn // LANE):
        blk = acc[:, hh * LANE:(hh + 1) * LANE]
        if norm:
            inv = lax.rsqrt(jnp.mean(blk * blk, axis=-1, keepdims=True) + EPS)
            blk = blk * jnp.where(is_v, 1.0, inv)
        blk = blk * cs_ref[:, hh * LANE:(hh + 1) * LANE]
        if rope:
            partner = jnp.where(first_half, pltpu.roll(blk, LANE - LANE // 4, 1),
                                pltpu.roll(blk, LANE // 4, 1))
            blk = blk * cos + partner * sin
        outs.append(blk)
    res = jnp.concatenate(outs, axis=1) if len(outs) > 1 else outs[0]
    o_ref[...] = res.astype(o_ref.dtype)
    if want_kv32:
        kv_ref[...] = res


def _qkv_proj(h, w, layer, colscale, q_dim, kv_dim, *, norm, rope_tables, want_kv32):
    m, d = h.shape
    n = w.shape[2]
    bm = _tile(m, TILES["qkv_m"], 16)
    bn = _tile(kv_dim, TILES["qkv_n"], LANE)
    assert q_dim % bn == 0 and kv_dim % bn == 0
    n_q, n_k = q_dim // bn, kv_dim // bn
    rope = rope_tables is not None
    in_specs = [
        pl.BlockSpec((bm, d), lambda i, j: (i, 0)),
        pl.BlockSpec((None, d, bn), lambda i, j: (layer, 0, j)),
        pl.BlockSpec((1, bn), lambda i, j: (0, j)),
    ]
    args = [h, w, colscale]
    if rope:
        cos, sin = rope_tables
        t = cos.shape[0]
        assert t % bm == 0
        per = t // bm
        in_specs += [pl.BlockSpec((bm, LANE), lambda i, j: (i % per, 0))] * 2
        args += [cos, sin]
    out_specs = [pl.BlockSpec((bm, bn), lambda i, j: (i, j))]
    out_shape = [jax.ShapeDtypeStruct((m, n), BF16)]
    if want_kv32:
        out_specs.append(pl.BlockSpec((bm, bn), lambda i, j: (i, jnp.maximum(j - n_q, 0))))
        out_shape.append(jax.ShapeDtypeStruct((m, 2 * kv_dim), F32))
    return pl.pallas_call(
        functools.partial(_qkv_kernel, n_q=n_q, n_k=n_k, norm=norm, rope=rope, want_kv32=want_kv32),
        grid=(m // bm, n // bn),
        in_specs=in_specs,
        out_specs=out_specs,
        out_shape=out_shape,
        compiler_params=_params("parallel", "arbitrary"),
        name="qkv_proj",
    )(*args)


def _stack_heads(q_ref, r_heads):
    return jnp.concatenate([q_ref[:, r * LANE:(r + 1) * LANE] for r in range(r_heads)], axis=0)


def _store_heads_t(o_ref, o_t, r_heads):
    tq = o_t.shape[1] // r_heads
    for r in range(r_heads):
        o_ref[:, r * LANE:(r + 1) * LANE] = o_t[:, r * tq:(r + 1) * tq].T.astype(o_ref.dtype)


def _max_key_norm(*k_refs):
    def max_sq_norm(ref):
        x = ref[...].astype(F32)
        return jnp.max(jnp.sum(x * x, axis=1, keepdims=True))
    return jnp.sqrt(functools.reduce(jnp.maximum, [max_sq_norm(r) for r in k_refs]))


def _sink_row(sink_ref, g, r_heads, tq):
    return jnp.concatenate(
        [jnp.full((1, tq), sink_ref[g * r_heads + r], F32) for r in range(r_heads)], axis=1)


def _attend(q_ref, o_ref, shift_ref, acc_ref, segments, k_max, sink, r_heads):
    qs = _stack_heads(q_ref, r_heads)
    rows = qs.shape[0]
    q_sq = _dot_nt(jnp.ones((SUB, LANE), BF16), qs * qs)[0:1]
    bound = jnp.sqrt(q_sq) * k_max
    if sink is not None:
        bound = jnp.maximum(bound, sink)
    shift_ref[...] = jnp.zeros_like(shift_ref)

    def attempt(carry):
        it, _ = carry
        shift = shift_ref[...]
        off = bound - shift
        den = jnp.zeros((1, rows), F32)
        acc = jnp.zeros((LANE, rows), F32)
        for k, v_t, bias in segments():
            s = _dot_nt(k, qs)
            if bias is not None:
                s = s + bias
            p = jnp.exp2(s - off)
            den = den + jnp.sum(p, axis=0, keepdims=True)
            acc = acc + _dot(v_t, p.astype(BF16))
        if sink is not None:
            den = den + jnp.exp2(sink - off)
        acc_ref[...] = acc * (1.0 / den)
        shift_ref[...] = jnp.where(den < UNDERFLOW_SUM, shift + RETRY_SHIFT, shift)
        return it + 1, jnp.min(den)

    def unfinished(carry):
        it, min_sum = carry
        return jnp.logical_or(it == 0, min_sum < UNDERFLOW_SUM)

    lax.while_loop(unfinished, attempt, (jnp.int32(0), jnp.float32(1.0)))
    _store_heads_t(o_ref, acc_ref[...], r_heads)


def _attn_scratch(rows):
    return [pltpu.SMEM((1,), F32), pltpu.VMEM((1, rows), F32), pltpu.VMEM((LANE, rows), F32)]


def _self_attn_kernel(sink_ref, q_ref, k_ref, vt_ref, o_ref, kmax_ref, shift_ref, acc_ref, *,
                      r_heads, has_sink):
    kmax_ref[0] = _max_key_norm(k_ref)
    sink = _sink_row(sink_ref, pl.program_id(1), r_heads, q_ref.shape[0]) if has_sink else None
    _attend(q_ref, o_ref, shift_ref, acc_ref, lambda: [(k_ref[...], vt_ref[...], None)],
            kmax_ref[0], sink, r_heads)


def _self_attention(qkv, v_t, sink, batch, q_dim, kv_dim, *, has_sink):
    m = qkv.shape[0]
    t = m // batch
    n_kv = kv_dim // LANE
    r_heads = q_dim // kv_dim
    qw = r_heads * LANE
    return pl.pallas_call(
        functools.partial(_self_attn_kernel, r_heads=r_heads, has_sink=has_sink),
        grid=(batch, n_kv),
        in_specs=[
            pl.BlockSpec(memory_space=pltpu.SMEM),
            pl.BlockSpec((t, qw), lambda b, g: (b, g)),
            pl.BlockSpec((t, LANE), lambda b, g: (b, q_dim // LANE + g)),
            pl.BlockSpec((LANE, t), lambda b, g: (g, b)),
        ],
        out_specs=pl.BlockSpec((t, qw), lambda b, g: (b, g)),
        out_shape=jax.ShapeDtypeStruct((m, q_dim), BF16),
        scratch_shapes=_attn_scratch(r_heads * t),
        compiler_params=_params("parallel", "parallel"),
        name="self_attention",
    )(sink, qkv, qkv, v_t)


def _win_attn_kernel(sink_ref, q_ref, k_ref, vt_ref, kc_ref, vct_ref, bias_ref, o_ref,
                     kmax_ref, shift_ref, acc_ref, *, r_heads, span):
    i = pl.program_id(2)
    tq = q_ref.shape[0]
    t = k_ref.shape[0]

    @pl.when(i == 0)
    def _():
        kmax_ref[0] = _max_key_norm(k_ref, kc_ref)

    base = pl.multiple_of(jnp.clip(i * tq - WINDOW, 0, t - span), LANE)

    def segments():
        k_w = k_ref[pl.ds(base, span), :]
        v_w_t = jnp.concatenate([vt_ref[base // LANE + c] for c in range(span // LANE)], axis=1)
        bias = jnp.concatenate([bias_ref[...]] * r_heads, axis=1)
        return [(kc_ref[...], vct_ref[...], None), (k_w, v_w_t, bias)]

    _attend(q_ref, o_ref, shift_ref, acc_ref, segments, kmax_ref[0],
            _sink_row(sink_ref, pl.program_id(1), r_heads, tq), r_heads)


def _window_attention(qkv, v_t, k_ctx, v_ctx_t, layer, sink, batch, q_dim, kv_dim):
    m = qkv.shape[0]
    t = m // batch
    n_kv = kv_dim // LANE
    r_heads = q_dim // kv_dim
    qw = r_heads * LANE
    tq = _tile(t, TILES["win_q"], LANE)
    reach = -(-WINDOW // LANE) * LANE
    span = tq + 2 * reach
    assert t % tq == 0 and t >= span
    nb = t // tq
    p = k_ctx.shape[3]
    blk = jnp.arange(nb, dtype=jnp.int32)[:, None, None]
    base = jnp.clip(blk * tq - reach, 0, t - span)
    kpos = base + jnp.arange(span, dtype=jnp.int32)[None, :, None]
    qpos = blk * tq + jnp.arange(tq, dtype=jnp.int32)[None, None, :]
    bias = jnp.where(jnp.abs(qpos - kpos) <= WINDOW, 0.0, NEG_BIG).astype(F32)
    return pl.pallas_call(
        functools.partial(_win_attn_kernel, r_heads=r_heads, span=span),
        grid=(batch, n_kv, nb),
        in_specs=[
            pl.BlockSpec(memory_space=pltpu.SMEM),
            pl.BlockSpec((tq, qw), lambda b, g, i: (b * nb + i, g)),
            pl.BlockSpec((t, LANE), lambda b, g, i: (b, q_dim // LANE + g)),
            pl.BlockSpec((None, t // LANE, LANE, LANE), lambda b, g, i: (g, b, 0, 0)),
            pl.BlockSpec((None, None, None, p, LANE), lambda b, g, i: (b, layer, g, 0, 0)),
            pl.BlockSpec((None, None, None, LANE, p), lambda b, g, i: (b, layer, g, 0, 0)),
            pl.BlockSpec((None, span, tq), lambda b, g, i: (i, 0, 0)),
        ],
        out_specs=pl.BlockSpec((tq, qw), lambda b, g, i: (b * nb + i, g)),
        out_shape=jax.ShapeDtypeStruct((m, q_dim), BF16),
        scratch_shapes=_attn_scratch(r_heads * tq),
        compiler_params=_params("parallel", "parallel", "arbitrary"),
        name="window_attention",
    )(sink, qkv, qkv, v_t, k_ctx, v_ctx_t, bias)


def _dense_attn_kernel(q_ref, k_ref, vt_ref, kc_ref, vct_ref, o_ref, kmax_ref, shift_ref, acc_ref, *,
                       r_heads, kv_chunk):
    t = k_ref.shape[0]

    @pl.when(pl.program_id(2) == 0)
    def _():
        kmax_ref[0] = _max_key_norm(k_ref, kc_ref)

    def segments():
        return [(kc_ref[...], vct_ref[...], None)] + [
            (k_ref[c * kv_chunk:(c + 1) * kv_chunk, :], vt_ref[:, c * kv_chunk:(c + 1) * kv_chunk], None)
            for c in range(t // kv_chunk)]

    _attend(q_ref, o_ref, shift_ref, acc_ref, segments, kmax_ref[0], None, r_heads)


def _dense_attention(qkv, v_t, k_ctx, v_ctx_t, layer, batch, q_dim, kv_dim):
    m = qkv.shape[0]
    t = m // batch
    n_kv = kv_dim // LANE
    r_heads = q_dim // kv_dim
    qw = r_heads * LANE
    tq = _tile(t, TILES["attn_q"], LANE)
    kv_chunk = _tile(t, TILES["attn_kv"], LANE)
    nb = t // tq
    p = k_ctx.shape[3]
    return pl.pallas_call(
        functools.partial(_dense_attn_kernel, r_heads=r_heads, kv_chunk=kv_chunk),
        grid=(batch, n_kv, nb),
        in_specs=[
            pl.BlockSpec((tq, qw), lambda b, g, i: (b * nb + i, g)),
            pl.BlockSpec((t, LANE), lambda b, g, i: (b, q_dim // LANE + g)),
            pl.BlockSpec((LANE, t), lambda b, g, i: (g, b)),
            pl.BlockSpec((None, None, None, p, LANE), lambda b, g, i: (b, layer, g, 0, 0)),
            pl.BlockSpec((None, None, None, LANE, p), lambda b, g, i: (b, layer, g, 0, 0)),
        ],
        out_specs=pl.BlockSpec((tq, qw), lambda b, g, i: (b * nb + i, g)),
        out_shape=jax.ShapeDtypeStruct((m, q_dim), BF16),
        scratch_shapes=_attn_scratch(r_heads * tq),
        compiler_params=_params("parallel", "parallel", "arbitrary"),
        name="dense_attention",
    )(qkv, qkv, v_t, k_ctx, v_ctx_t)


def _proj_res_kernel(a_ref, w_ref, x_ref, g_ref, o_ref):
    o_ref[...] = x_ref[...] + g_ref[...] * _dot(a_ref[...], w_ref[...])


def _proj_residual(a, w, layer, x, mod, gate_slot, bm_pref, bn_pref):
    m, k = a.shape
    d = w.shape[2]
    bm = _tile(min(m, mod.rows_per_group), bm_pref, 16)
    bn = _tile(d, bn_pref, LANE)
    nj = d // bn
    return pl.pallas_call(
        _proj_res_kernel,
        grid=(m // bm, nj),
        in_specs=[
            pl.BlockSpec((bm, k), lambda i, j: (i, 0)),
            pl.BlockSpec((None, k, bn), lambda i, j: (layer, 0, j)),
            pl.BlockSpec((bm, bn), lambda i, j: (i, j)),
            mod.spec(gate_slot, bm, bn, nj),
        ],
        out_specs=pl.BlockSpec((bm, bn), lambda i, j: (i, j)),
        out_shape=jax.ShapeDtypeStruct((m, d), F32),
        compiler_params=_params("parallel", "parallel"),
        name="proj_residual",
    )(a, w, x, mod.table)


def _matmul_kernel(a_ref, w_ref, o_ref):
    o_ref[...] = _dot(a_ref[...], w_ref[...])


def _edge_rows_kernel(before_ref, after_ref, o_ref):
    row = lax.broadcasted_iota(jnp.int32, o_ref.shape, 0)
    last = before_ref[...].astype(F32)[2 * SUB - 1:2 * SUB, :]
    first = after_ref[...].astype(F32)[0:1, :]
    o_ref[...] = jnp.where(row == SUB - 1, last, jnp.where(row == SUB, first, 0.0)).astype(o_ref.dtype)


def _edge_gate(h, wg, layer, bm):
    m, d = h.shape
    nt = m // bm
    d_ff = wg.shape[2]
    per = bm // (2 * SUB)
    edge = pl.pallas_call(
        _edge_rows_kernel,
        grid=(nt,),
        in_specs=[pl.BlockSpec((2 * SUB, d), lambda i: (jnp.maximum(i * per - 1, 0), 0)),
                  pl.BlockSpec((2 * SUB, d), lambda i: (jnp.minimum((i + 1) * per, nt * per - 1), 0))],
        out_specs=pl.BlockSpec((None, 2 * SUB, d), lambda i: (i, 0, 0)),
        out_shape=jax.ShapeDtypeStruct((nt, 2 * SUB, d), h.dtype),
        compiler_params=_params("parallel"),
        name="edge_rows",
    )(h, h)
    bn = min(d_ff, TILES["ffn_f"])
    out = pl.pallas_call(
        _matmul_kernel,
        grid=(pl.cdiv(d_ff, bn),),
        in_specs=[pl.BlockSpec((nt * 2 * SUB, d), lambda j: (0, 0)),
                  pl.BlockSpec((None, d, bn), lambda j: (layer, 0, j))],
        out_specs=pl.BlockSpec((nt * 2 * SUB, bn), lambda j: (0, j)),
        out_shape=jax.ShapeDtypeStruct((nt * 2 * SUB, d_ff), F32),
        compiler_params=_params("parallel"),
        name="edge_gate",
    )(edge.reshape(nt * 2 * SUB, d), wg)
    return out.reshape(nt, 2 * SUB, d_ff)


def _gate_up_kernel(h_ref, wg_ref, wu_ref, cw_ref, cb_ref, e_ref, o_ref, a_ref, *, bm, seq_len):
    a_ref[0:SUB, :] = e_ref[0:SUB, :]
    a_ref[SUB:SUB + bm, :] = _dot(h_ref[...], wg_ref[...])
    a_ref[SUB + bm:, :] = e_ref[SUB:, :]
    u = _dot(h_ref[...], wu_ref[...])
    row = pl.program_id(0) * bm + lax.broadcasted_iota(jnp.int32, (bm, 1), 0)
    pos = row % seq_len
    prev = jnp.where(pos == 0, 0.0, a_ref[SUB - 1:SUB - 1 + bm, :])
    nxt = jnp.where(pos == seq_len - 1, 0.0, a_ref[SUB + 1:SUB + 1 + bm, :])
    cur = a_ref[SUB:SUB + bm, :]
    a = prev * cw_ref[0:1, :] + cur * cw_ref[1:2, :] + nxt * cw_ref[2:3, :] + cb_ref[...]
    o_ref[...] = (_silu(a) * u).astype(o_ref.dtype)


def _gate_up(h, wg, wu, conv_w, conv_b, layer, seq_len, bm):
    m, d = h.shape
    d_ff = wg.shape[2]
    bf = min(d_ff, TILES["ffn_f"])
    edge = _edge_gate(h, wg, layer, bm)
    return pl.pallas_call(
        functools.partial(_gate_up_kernel, bm=bm, seq_len=seq_len),
        grid=(m // bm, pl.cdiv(d_ff, bf)),
        in_specs=[
            pl.BlockSpec((bm, d), lambda i, f: (i, 0)),
            pl.BlockSpec((None, d, bf), lambda i, f: (layer, 0, f)),
            pl.BlockSpec((None, d, bf), lambda i, f: (layer, 0, f)),
            pl.BlockSpec((None, conv_w.shape[1], bf), lambda i, f: (layer, 0, f)),
            pl.BlockSpec((None, 1, bf), lambda i, f: (layer, 0, f)),
            pl.BlockSpec((None, 2 * SUB, bf), lambda i, f: (i, 0, f)),
        ],
        out_specs=pl.BlockSpec((bm, bf), lambda i, f: (i, f)),
        out_shape=jax.ShapeDtypeStruct((m, d_ff), BF16),
        scratch_shapes=[pltpu.VMEM((bm + 2 * SUB, bf), F32)],
        compiler_params=_params("parallel", "parallel"),
        name="gate_up",
    )(h, wg, wu, conv_w, conv_b, edge)


def _rope_tables(n_tokens):
    axis_dim = LANE // 2
    half = axis_dim // 2
    rows = n_tokens // GRID_W
    row = jnp.repeat(jnp.arange(rows, dtype=jnp.int32), GRID_W)
    col = jnp.tile(jnp.arange(GRID_W, dtype=jnp.int32), rows)
    inv_freq = ROPE_THETA ** (-jnp.arange(half, dtype=F32) / half)
    cos_parts, sin_parts = [], []
    for pos in (row, col):
        ang = pos.astype(F32)[:, None] * inv_freq[None, :]
        c, s = jnp.cos(ang), jnp.sin(ang)
        cos_parts += [c, c]
        sin_parts += [-s, s]
    return jnp.concatenate(cos_parts, axis=1), jnp.concatenate(sin_parts, axis=1)


def kernel(x_prompt, x_sample, cache_k, cache_v, c, c_ctx, w_mod, b_mod, norm_attn, norm_ffn, w_qkv, w_o,
           sink_a, q_norm_b, k_norm_b, w_gate, w_up, w_down, conv_w, conv_b, norm_f):
    batch_p, seq_p, d = x_prompt.shape
    batch_s, seq_s, _ = x_sample.shape
    depth = w_mod.shape[0]
    n_kv, head_dim = cache_k.shape[3], cache_k.shape[4]
    n_heads = sink_a.shape[1]
    assert head_dim == LANE
    q_dim, kv_dim = n_heads * head_dim, n_kv * head_dim
    d_ff = w_gate.shape[2]
    scale = head_dim ** -0.5
    m_p, m_s = batch_p * seq_p, batch_s * seq_s

    ffn_m_p = _tile(m_p, TILES["ffn_m"], seq_p)
    ffn_m_s = _tile(seq_s, TILES["ffn_m"], 16)

    xp = x_prompt.reshape(m_p, d)
    xs = x_sample.reshape(m_s, d)

    n_cond = -(-(batch_s + 1) // SUB) * SUB
    cond = jnp.zeros((n_cond, d), F32).at[:batch_s].set(c).at[batch_s].set(c_ctx)
    mod_table = _modulation(cond, w_mod, b_mod).reshape(depth * n_cond, 1, -1)

    wqkv = w_qkv.astype(BF16)
    wo = w_o.astype(BF16)
    wg = w_gate.astype(BF16)
    wu = w_up.astype(BF16)
    wd = w_down.astype(BF16)
    cw = conv_w
    cb = conv_b.reshape(depth, 1, d_ff)
    gains_attn = norm_attn.reshape(depth, 1, d)
    gains_ffn = norm_ffn.reshape(depth, 1, d)
    k_ctx = cache_k.transpose(0, 1, 3, 2, 4).astype(BF16)
    v_ctx_t = cache_v.transpose(0, 1, 3, 4, 2).astype(BF16)

    rope_tables = _rope_tables(seq_s)
    ones_kv = jnp.ones((kv_dim,), F32)
    ks_out, vs_out = [], []
    for i in range(depth):
        mod_s = _Mod(mod_table, i * n_cond, seq_s)
        mod_p = _Mod(mod_table, i * n_cond + batch_s, m_p)
        mixer_a = i % 2 == 0
        if mixer_a:
            sink = sink_a[i // 2].astype(F32) * LOG2E
            colscale = jnp.concatenate([jnp.full((q_dim,), scale * LOG2E, F32), ones_kv, ones_kv])
        else:
            sink = jnp.zeros((n_heads,), F32)
            colscale = jnp.concatenate([jnp.tile(q_norm_b[i // 2].astype(F32), n_heads) * (scale * LOG2E),
                                        jnp.tile(k_norm_b[i // 2].astype(F32), n_kv), ones_kv])
        colscale = colscale.reshape(1, -1)

        hp = _norm_mod(xp, gains_attn, i, mod_p, 0)
        hs = _norm_mod(xs, gains_attn, i, mod_s, 0)
        qkv_p, kv32 = _qkv_proj(hp, wqkv, i, colscale, q_dim, kv_dim, norm=not mixer_a, rope_tables=None,
                                want_kv32=True)
        qkv_s, = _qkv_proj(hs, wqkv, i, colscale, q_dim, kv_dim, norm=not mixer_a, rope_tables=rope_tables,
                           want_kv32=False)
        ks_out.append(kv32[:, :kv_dim].reshape(batch_p, seq_p, n_kv, head_dim))
        vs_out.append(kv32[:, kv_dim:].reshape(batch_p, seq_p, n_kv, head_dim))
        op = _self_attention(qkv_p, qkv_p[:, q_dim + kv_dim:].T, sink, batch_p, q_dim, kv_dim,
                             has_sink=mixer_a)
        v_s = qkv_s[:, q_dim + kv_dim:]
        if mixer_a:
            v_t = v_s.reshape(m_s // LANE, LANE, n_kv, LANE).transpose(2, 0, 3, 1)
            osm = _window_attention(qkv_s, v_t, k_ctx, v_ctx_t, i, sink, batch_s, q_dim, kv_dim)
        else:
            osm = _dense_attention(qkv_s, v_s.T, k_ctx, v_ctx_t, i, batch_s, q_dim, kv_dim)
        xp = _proj_residual(op, wo, i, xp, mod_p, 2, TILES["proj_m"], TILES["proj_n"])
        xs = _proj_residual(osm, wo, i, xs, mod_s, 2, TILES["proj_m"], TILES["proj_n"])

        hp = _norm_mod(xp, gains_ffn, i, mod_p, 3)
        hs = _norm_mod(xs, gains_ffn, i, mod_s, 3)
        zp = _gate_up(hp, wg, wu, cw, cb, i, seq_p, ffn_m_p)
        zs = _gate_up(hs, wg, wu, cw, cb, i, seq_s, ffn_m_s)
        xp = _proj_residual(zp, wd, i, xp, mod_p, 5, TILES["down_m"], TILES["down_n"])
        xs = _proj_residual(zs, wd, i, xs, mod_s, 5, TILES["down_m"], TILES["down_n"])

    y_prompt = _final_norm(xp, norm_f).reshape(batch_p, seq_p, d)
    y_sample = _final_norm(xs, norm_f).reshape(batch_s, seq_s, d)
    return y_prompt, y_sample, jnp.stack(ks_out, axis=1), jnp.stack(vs_out, axis=1)
```

```python
import functools
import math

import jax
import jax.numpy as jnp
from jax import lax
from jax.experimental import pallas as pl
from jax.experimental.pallas import tpu as pltpu

GRID_W = 64
WINDOW = 128
ROPE_THETA = 10000.0
EPS = 1e-6
LANE = 128
SUB = 8
NEG_BIG = -1e30
UNDERFLOW_SUM = 2.0 ** -60
RETRY_SHIFT = 64.0
LOG2E = math.log2(math.e)
VMEM_LIMIT_BYTES = 58 * 1024 * 1024

BF16 = jnp.bfloat16
F32 = jnp.float32

TILES = dict(
    mod_n=512,
    norm_m=512,
    qkv_m=1024, qkv_n=1024,
    proj_m=1024, proj_n=1024,
    ffn_m=1024, ffn_f=512,
    down_m=512, down_n=512,
    win_q=256,
    attn_q=256, attn_kv=2048,
)


def _tile(dim, pref, mult):
    if dim <= pref:
        return dim
    t = pref - pref % mult
    while t >= mult:
        if dim % t == 0:
            return t
        t -= mult
    return dim


def _params(*semantics):
    return pltpu.CompilerParams(dimension_semantics=semantics, vmem_limit_bytes=VMEM_LIMIT_BYTES)


def _silu(x):
    return x / (1.0 + jnp.exp(-x))


def _dot(a, b):
    return jnp.dot(a, b, preferred_element_type=F32)


def _dot_nt(a, b):
    return lax.dot_general(a, b, (((1,), (1,)), ((), ())), preferred_element_type=F32)


def _mod_kernel(c_ref, w_ref, b_ref, o_ref):
    a = _silu(c_ref[...]).astype(BF16)
    o_ref[...] = _dot(a, w_ref[...].astype(BF16)) + b_ref[...]


def _modulation(cond, w_mod, b_mod):
    depth, d, n = w_mod.shape
    rows = cond.shape[0]
    tn = _tile(n, TILES["mod_n"], LANE)
    return pl.pallas_call(
        _mod_kernel,
        grid=(depth, n // tn),
        in_specs=[
            pl.BlockSpec((rows, d), lambda l, j: (0, 0)),
            pl.BlockSpec((None, d, tn), lambda l, j: (l, 0, j)),
            pl.BlockSpec((None, 1, tn), lambda l, j: (l, 0, j)),
        ],
        out_specs=pl.BlockSpec((None, rows, tn), lambda l, j: (l, 0, j)),
        out_shape=jax.ShapeDtypeStruct((depth, rows, n), F32),
        compiler_params=_params("parallel", "parallel"),
        name="modulation",
    )(cond, w_mod, b_mod.reshape(depth, 1, n))


class _Mod:
    def __init__(self, table, base, rows_per_group):
        self.table, self.base, self.rows_per_group = table, base, rows_per_group

    def spec(self, slot, bm, bn, nj):
        base, rpg = self.base, self.rows_per_group
        if nj == 1:
            return pl.BlockSpec((None, 1, bn), lambda i, *_: (base + (i * bm) // rpg, 0, slot))
        return pl.BlockSpec((None, 1, bn), lambda i, j: (base + (i * bm) // rpg, 0, slot * nj + j))


NORM_ROWS = 16
NORM_UNROLL = 8


def _rms_rows(x_ref, o_ref, scale_ref, shift_ref):
    def body(r, carry):
        rows = pl.ds(pl.multiple_of(r * NORM_ROWS, NORM_ROWS), NORM_ROWS)
        x = x_ref[rows, :]
        inv = lax.rsqrt(jnp.mean(x * x, axis=-1, keepdims=True) + EPS)
        y = x_ref[rows, :] * inv * scale_ref[...]
        if shift_ref is not None:
            y = y + shift_ref[...]
        o_ref[rows, :] = y.astype(o_ref.dtype)
        return carry

    lax.fori_loop(0, x_ref.shape[0] // NORM_ROWS, body, 0, unroll=NORM_UNROLL)


def _norm_mod_kernel(x_ref, g_ref, sh_ref, sc_ref, o_ref, gs_ref):
    gs_ref[...] = g_ref[...] * (1.0 + sc_ref[...])
    _rms_rows(x_ref, o_ref, gs_ref, sh_ref)


def _norm_mod(x, gains, layer, mod, shift_slot):
    m, d = x.shape
    bm = _tile(min(m, mod.rows_per_group), TILES["norm_m"], NORM_ROWS)
    return pl.pallas_call(
        _norm_mod_kernel,
        grid=(m // bm,),
        in_specs=[
            pl.BlockSpec((bm, d), lambda i: (i, 0)),
            pl.BlockSpec((None, 1, d), lambda i: (layer, 0, 0)),
            mod.spec(shift_slot, bm, d, 1),
            mod.spec(shift_slot + 1, bm, d, 1),
        ],
        out_specs=pl.BlockSpec((bm, d), lambda i: (i, 0)),
        out_shape=jax.ShapeDtypeStruct((m, d), BF16),
        scratch_shapes=[pltpu.VMEM((1, d), F32)],
        compiler_params=_params("parallel"),
        name="norm_mod",
    )(x, gains, mod.table, mod.table)


def _norm_kernel(x_ref, g_ref, o_ref):
    _rms_rows(x_ref, o_ref, g_ref, None)


def _final_norm(x, gain):
    m, d = x.shape
    bm = _tile(m, TILES["norm_m"], NORM_ROWS)
    return pl.pallas_call(
        _norm_kernel,
        grid=(m // bm,),
        in_specs=[pl.BlockSpec((bm, d), lambda i: (i, 0)), pl.BlockSpec((1, d), lambda i: (0, 0))],
        out_specs=pl.BlockSpec((bm, d), lambda i: (i, 0)),
        out_shape=jax.ShapeDtypeStruct((m, d), F32),
        compiler_params=_params("parallel"),
        name="final_norm",
    )(x, gain.reshape(1, d))


def _qkv_kernel(*refs, n_q, n_k, norm, rope, want_kv32):
    h_ref, w_ref, cs_ref = refs[:3]
    refs = refs[3:]
    if rope:
        cos_ref, sin_ref = refs[:2]
        refs = refs[2:]
    o_ref = refs[0]
    kv_ref = refs[1] if want_kv32 else None
    is_v = pl.program_id(1) >= n_q + n_k
    acc = _dot(h_ref[...], w_ref[...])
    bn = acc.shape[1]
    if rope:
        cos = jnp.where(is_v, 1.0, cos_ref[...])
        sin = jnp.where(is_v, 0.0, sin_ref[...])
        lane = lax.broadcasted_iota(jnp.int32, (1, LANE), 1)
        first_half = (lane % (LANE // 2)) < (LANE // 4)
    outs = []
    for hh in range(bn // LANE):
        blk = acc[:, hh * LANE:(hh + 1) * LANE]
        if norm:
            inv = lax.rsqrt(jnp.mean(blk * blk, axis=-1, keepdims=True) + EPS)
            blk = blk * jnp.where(is_v, 1.0, inv)
        blk = blk * cs_ref[:, hh * LANE:(hh + 1) * LANE]
        if rope:
            partner = jnp.where(first_half, pltpu.roll(blk, LANE - LANE // 4, 1),
                                pltpu.roll(blk, LANE // 4, 1))
            blk = blk * cos + partner * sin
        outs.append(blk)
    res = jnp.concatenate(outs, axis=1) if len(outs) > 1 else outs[0]
    o_ref[...] = res.astype(o_ref.dtype)
    if want_kv32:
        kv_ref[...] = res


def _qkv_proj(h, w, layer, colscale, q_dim, kv_dim, *, norm, rope_tables, want_kv32):
    m, d = h.shape
    n = w.shape[2]
    bm = _tile(m, TILES["qkv_m"], 16)
    bn = _tile(kv_dim, TILES["qkv_n"], LANE)
    assert q_dim % bn == 0 and kv_dim % bn == 0
    n_q, n_k = q_dim // bn, kv_dim // bn
    rope = rope_tables is not None
    in_specs = [
        pl.BlockSpec((bm, d), lambda i, j: (i, 0)),
        pl.BlockSpec((None, d, bn), lambda i, j: (layer, 0, j)),
        pl.BlockSpec((1, bn), lambda i, j: (0, j)),
    ]
    args = [h, w, colscale]
    if rope:
        cos, sin = rope_tables
        t = cos.shape[0]
        assert t % bm == 0
        per = t // bm
        in_specs += [pl.BlockSpec((bm, LANE), lambda i, j: (i % per, 0))] * 2
        args += [cos, sin]
    out_specs = [pl.BlockSpec((bm, bn), lambda i, j: (i, j))]
    out_shape = [jax.ShapeDtypeStruct((m, n), BF16)]
    if want_kv32:
        out_specs.append(pl.BlockSpec((bm, bn), lambda i, j: (i, jnp.maximum(j - n_q, 0))))
        out_shape.append(jax.ShapeDtypeStruct((m, 2 * kv_dim), F32))
    return pl.pallas_call(
        functools.partial(_qkv_kernel, n_q=n_q, n_k=n_k, norm=norm, rope=rope, want_kv32=want_kv32),
        grid=(m // bm, n // bn),
        in_specs=in_specs,
        out_specs=out_specs,
        out_shape=out_shape,
        compiler_params=_params("parallel", "arbitrary"),
        name="qkv_proj",
    )(*args)


def _stack_heads(q_ref, r_heads):
    return jnp.concatenate([q_ref[:, r * LANE:(r + 1) * LANE] for r in range(r_heads)], axis=0)


def _store_heads_t(o_ref, o_t, r_heads):
    tq = o_t.shape[1] // r_heads
    for r in range(r_heads):
        o_ref[:, r * LANE:(r + 1) * LANE] = o_t[:, r * tq:(r + 1) * tq].T.astype(o_ref.dtype)


def _max_key_norm(*k_refs):
    def max_sq_norm(ref):
        x = ref[...].astype(F32)
        return jnp.max(jnp.sum(x * x, axis=1, keepdims=True))
    return jnp.sqrt(functools.reduce(jnp.maximum, [max_sq_norm(r) for r in k_refs]))


def _sink_row(sink_ref, g, r_heads, tq):
    return jnp.concatenate(
        [jnp.full((1, tq), sink_ref[g * r_heads + r], F32) for r in range(r_heads)], axis=1)


def _attend(q_ref, o_ref, shift_ref, acc_ref, segments, k_max, sink, r_heads):
    qs = _stack_heads(q_ref, r_heads)
    rows = qs.shape[0]
    q_sq = _dot_nt(jnp.ones((SUB, LANE), BF16), qs * qs)[0:1]
    bound = jnp.sqrt(q_sq) * k_max
    if sink is not None:
        bound = jnp.maximum(bound, sink)

    def attempt(off):
        den = jnp.zeros((1, rows), F32)
        acc = jnp.zeros((LANE, rows), F32)
        for k, v_t, bias in segments():
            s = _dot_nt(k, qs)
            if bias is not None:
                s = s + bias
            p = jnp.exp2(s - off)
            den = den + jnp.sum(p, axis=0, keepdims=True)
            acc = acc + _dot(v_t, p.astype(BF16))
        if sink is not None:
            den = den + jnp.exp2(sink - off)
        return acc * (1.0 / den), den

    out, den = attempt(bound)
    _store_heads_t(o_ref, out, r_heads)

    @pl.when(jnp.min(den) < UNDERFLOW_SUM)
    def _():
        shift_ref[...] = jnp.where(den < UNDERFLOW_SUM, RETRY_SHIFT, 0.0)

        def retry(_):
            shift = shift_ref[...]
            out, den = attempt(bound - shift)
            acc_ref[...] = out
            shift_ref[...] = jnp.where(den < UNDERFLOW_SUM, shift + RETRY_SHIFT, shift)
            return jnp.min(den)

        lax.while_loop(lambda min_sum: min_sum < UNDERFLOW_SUM, retry, jnp.float32(0.0))
        _store_heads_t(o_ref, acc_ref[...], r_heads)


def _attn_scratch(rows):
    return [pltpu.SMEM((1,), F32), pltpu.VMEM((1, rows), F32), pltpu.VMEM((LANE, rows), F32)]


def _self_attn_kernel(sink_ref, q_ref, k_ref, vt_ref, o_ref, kmax_ref, shift_ref, acc_ref, *,
                      r_heads, has_sink):
    kmax_ref[0] = _max_key_norm(k_ref)
    sink = _sink_row(sink_ref, pl.program_id(1), r_heads, q_ref.shape[0]) if has_sink else None
    _attend(q_ref, o_ref, shift_ref, acc_ref, lambda: [(k_ref[...], vt_ref[...], None)],
            kmax_ref[0], sink, r_heads)


def _self_attention(qkv, v_t, sink, batch, q_dim, kv_dim, *, has_sink):
    m = qkv.shape[0]
    t = m // batch
    n_kv = kv_dim // LANE
    r_heads = q_dim // kv_dim
    qw = r_heads * LANE
    return pl.pallas_call(
        functools.partial(_self_attn_kernel, r_heads=r_heads, has_sink=has_sink),
        grid=(batch, n_kv),
        in_specs=[
            pl.BlockSpec(memory_space=pltpu.SMEM),
            pl.BlockSpec((t, qw), lambda b, g: (b, g)),
            pl.BlockSpec((t, LANE), lambda b, g: (b, q_dim // LANE + g)),
            pl.BlockSpec((LANE, t), lambda b, g: (g, b)),
        ],
        out_specs=pl.BlockSpec((t, qw), lambda b, g: (b, g)),
        out_shape=jax.ShapeDtypeStruct((m, q_dim), BF16),
        scratch_shapes=_attn_scratch(r_heads * t),
        compiler_params=_params("parallel", "parallel"),
        name="self_attention",
    )(sink, qkv, qkv, v_t)


def _win_attn_kernel(sink_ref, q_ref, k_ref, vt_ref, kc_ref, vct_ref, bias_ref, o_ref,
                     kmax_ref, shift_ref, acc_ref, *, r_heads, span):
    i = pl.program_id(2)
    tq = q_ref.shape[0]
    t = k_ref.shape[0]

    @pl.when(i == 0)
    def _():
        kmax_ref[0] = _max_key_norm(k_ref, kc_ref)

    base = pl.multiple_of(jnp.clip(i * tq - WINDOW, 0, t - span), LANE)

    def segments():
        k_w = k_ref[pl.ds(base, span), :]
        v_w_t = jnp.concatenate([vt_ref[base // LANE + c] for c in range(span // LANE)], axis=1)
        bias = jnp.concatenate([bias_ref[...]] * r_heads, axis=1)
        return [(kc_ref[...], vct_ref[...], None), (k_w, v_w_t, bias)]

    _attend(q_ref, o_ref, shift_ref, acc_ref, segments, kmax_ref[0],
            _sink_row(sink_ref, pl.program_id(1), r_heads, tq), r_heads)


def _window_attention(qkv, v_t, k_ctx, v_ctx_t, layer, sink, batch, q_dim, kv_dim):
    m = qkv.shape[0]
    t = m // batch
    n_kv = kv_dim // LANE
    r_heads = q_dim // kv_dim
    qw = r_heads * LANE
    tq = _tile(t, TILES["win_q"], LANE)
    reach = -(-WINDOW // LANE) * LANE
    span = tq + 2 * reach
    assert t % tq == 0 and t >= span
    nb = t // tq
    p = k_ctx.shape[3]
    blk = jnp.arange(nb, dtype=jnp.int32)[:, None, None]
    base = jnp.clip(blk * tq - reach, 0, t - span)
    kpos = base + jnp.arange(span, dtype=jnp.int32)[None, :, None]
    qpos = blk * tq + jnp.arange(tq, dtype=jnp.int32)[None, None, :]
    bias = jnp.where(jnp.abs(qpos - kpos) <= WINDOW, 0.0, NEG_BIG).astype(F32)
    return pl.pallas_call(
        functools.partial(_win_attn_kernel, r_heads=r_heads, span=span),
        grid=(batch, n_kv, nb),
        in_specs=[
            pl.BlockSpec(memory_space=pltpu.SMEM),
            pl.BlockSpec((tq, qw), lambda b, g, i: (b * nb + i, g)),
            pl.BlockSpec((t, LANE), lambda b, g, i: (b, q_dim // LANE + g)),
            pl.BlockSpec((None, t // LANE, LANE, LANE), lambda b, g, i: (g, b, 0, 0)),
            pl.BlockSpec((None, None, None, p, LANE), lambda b, g, i: (b, layer, g, 0, 0)),
            pl.BlockSpec((None, None, None, LANE, p), lambda b, g, i: (b, layer, g, 0, 0)),
            pl.BlockSpec((None, span, tq), lambda b, g, i: (i, 0, 0)),
        ],
        out_specs=pl.BlockSpec((tq, qw), lambda b, g, i: (b * nb + i, g)),
        out_shape=jax.ShapeDtypeStruct((m, q_dim), BF16),
        scratch_shapes=_attn_scratch(r_heads * tq),
        compiler_params=_params("parallel", "parallel", "arbitrary"),
        name="window_attention",
    )(sink, qkv, qkv, v_t, k_ctx, v_ctx_t, bias)


def _dense_attn_kernel(q_ref, k_ref, vt_ref, kc_ref, vct_ref, o_ref, kmax_ref, shift_ref, acc_ref, *,
                       r_heads, kv_chunk):
    t = k_ref.shape[0]

    @pl.when(pl.program_id(2) == 0)
    def _():
        kmax_ref[0] = _max_key_norm(k_ref, kc_ref)

    def segments():
        return [(kc_ref[...], vct_ref[...], None)] + [
            (k_ref[c * kv_chunk:(c + 1) * kv_chunk, :], vt_ref[:, c * kv_chunk:(c + 1) * kv_chunk], None)
            for c in range(t // kv_chunk)]

    _attend(q_ref, o_ref, shift_ref, acc_ref, segments, kmax_ref[0], None, r_heads)


def _dense_attention(qkv, v_t, k_ctx, v_ctx_t, layer, batch, q_dim, kv_dim):
    m = qkv.shape[0]
    t = m // batch
    n_kv = kv_dim // LANE
    r_heads = q_dim // kv_dim
    qw = r_heads * LANE
    tq = _tile(t, TILES["attn_q"], LANE)
    kv_chunk = _tile(t, TILES["attn_kv"], LANE)
    nb = t // tq
    p = k_ctx.shape[3]
    return pl.pallas_call(
        functools.partial(_dense_attn_kernel, r_heads=r_heads, kv_chunk=kv_chunk),
        grid=(batch, n_kv, nb),
        in_specs=[
            pl.BlockSpec((tq, qw), lambda b, g, i: (b * nb + i, g)),
            pl.BlockSpec((t, LANE), lambda b, g, i: (b, q_dim // LANE + g)),
            pl.BlockSpec((LANE, t), lambda b, g, i: (g, b)),
            pl.BlockSpec((None, None, None, p, LANE), lambda b, g, i: (b, layer, g, 0, 0)),
            pl.BlockSpec((None, None, None, LANE, p), lambda b, g, i: (b, layer, g, 0, 0)),
        ],
        out_specs=pl.BlockSpec((tq, qw), lambda b, g, i: (b * nb + i, g)),
        out_shape=jax.ShapeDtypeStruct((m, q_dim), BF16),
        scratch_shapes=_attn_scratch(r_heads * tq),
        compiler_params=_params("parallel", "parallel", "arbitrary"),
        name="dense_attention",
    )(qkv, qkv, v_t, k_ctx, v_ctx_t)


def _proj_res_kernel(a_ref, w_ref, x_ref, g_ref, o_ref):
    o_ref[...] = x_ref[...] + g_ref[...] * _dot(a_ref[...], w_ref[...])


def _proj_residual(a, w, layer, x, mod, gate_slot, bm_pref, bn_pref):
    m, k = a.shape
    d = w.shape[2]
    bm = _tile(min(m, mod.rows_per_group), bm_pref, 16)
    bn = _tile(d, bn_pref, LANE)
    nj = d // bn
    return pl.pallas_call(
        _proj_res_kernel,
        grid=(m // bm, nj),
        in_specs=[
            pl.BlockSpec((bm, k), lambda i, j: (i, 0)),
            pl.BlockSpec((None, k, bn), lambda i, j: (layer, 0, j)),
            pl.BlockSpec((bm, bn), lambda i, j: (i, j)),
            mod.spec(gate_slot, bm, bn, nj),
        ],
        out_specs=pl.BlockSpec((bm, bn), lambda i, j: (i, j)),
        out_shape=jax.ShapeDtypeStruct((m, d), F32),
        compiler_params=_params("parallel", "parallel"),
        name="proj_residual",
    )(a, w, x, mod.table)


def _matmul_kernel(a_ref, w_ref, o_ref):
    o_ref[...] = _dot(a_ref[...], w_ref[...])


def _edge_rows_kernel(before_ref, after_ref, o_ref):
    row = lax.broadcasted_iota(jnp.int32, o_ref.shape, 0)
    last = before_ref[...].astype(F32)[2 * SUB - 1:2 * SUB, :]
    first = after_ref[...].astype(F32)[0:1, :]
    o_ref[...] = jnp.where(row == SUB - 1, last, jnp.where(row == SUB, first, 0.0)).astype(o_ref.dtype)


def _edge_gate(h, wg, layer, bm):
    m, d = h.shape
    nt = m // bm
    d_ff = wg.shape[2]
    per = bm // (2 * SUB)
    edge = pl.pallas_call(
        _edge_rows_kernel,
        grid=(nt,),
        in_specs=[pl.BlockSpec((2 * SUB, d), lambda i: (jnp.maximum(i * per - 1, 0), 0)),
                  pl.BlockSpec((2 * SUB, d), lambda i: (jnp.minimum((i + 1) * per, nt * per - 1), 0))],
        out_specs=pl.BlockSpec((None, 2 * SUB, d), lambda i: (i, 0, 0)),
        out_shape=jax.ShapeDtypeStruct((nt, 2 * SUB, d), h.dtype),
        compiler_params=_params("parallel"),
        name="edge_rows",
    )(h, h)
    bn = min(d_ff, TILES["ffn_f"])
    out = pl.pallas_call(
        _matmul_kernel,
        grid=(pl.cdiv(d_ff, bn),),
        in_specs=[pl.BlockSpec((nt * 2 * SUB, d), lambda j: (0, 0)),
                  pl.BlockSpec((None, d, bn), lambda j: (layer, 0, j))],
        out_specs=pl.BlockSpec((nt * 2 * SUB, bn), lambda j: (0, j)),
        out_shape=jax.ShapeDtypeStruct((nt * 2 * SUB, d_ff), F32),
        compiler_params=_params("parallel"),
        name="edge_gate",
    )(edge.reshape(nt * 2 * SUB, d), wg)
    return out.reshape(nt, 2 * SUB, d_ff)


def _gate_up_kernel(h_ref, wg_ref, wu_ref, cw_ref, cb_ref, e_ref, o_ref, a_ref, *, bm, seq_len):
    a_ref[0:SUB, :] = e_ref[0:SUB, :]
    a_ref[SUB:SUB + bm, :] = _dot(h_ref[...], wg_ref[...])
    a_ref[SUB + bm:, :] = e_ref[SUB:, :]
    u = _dot(h_ref[...], wu_ref[...])
    row = pl.program_id(0) * bm + lax.broadcasted_iota(jnp.int32, (bm, 1), 0)
    pos = row % seq_len
    prev = jnp.where(pos == 0, 0.0, a_ref[SUB - 1:SUB - 1 + bm, :])
    nxt = jnp.where(pos == seq_len - 1, 0.0, a_ref[SUB + 1:SUB + 1 + bm, :])
    cur = a_ref[SUB:SUB + bm, :]
    a = prev * cw_ref[0:1, :] + cur * cw_ref[1:2, :] + nxt * cw_ref[2:3, :] + cb_ref[...]
    o_ref[...] = (_silu(a) * u).astype(o_ref.dtype)


def _gate_up(h, wg, wu, conv_w, conv_b, layer, seq_len, bm):
    m, d = h.shape
    d_ff = wg.shape[2]
    bf = min(d_ff, TILES["ffn_f"])
    edge = _edge_gate(h, wg, layer, bm)
    return pl.pallas_call(
        functools.partial(_gate_up_kernel, bm=bm, seq_len=seq_len),
        grid=(m // bm, pl.cdiv(d_ff, bf)),
        in_specs=[
            pl.BlockSpec((bm, d), lambda i, f: (i, 0)),
            pl.BlockSpec((None, d, bf), lambda i, f: (layer, 0, f)),
            pl.BlockSpec((None, d, bf), lambda i, f: (layer, 0, f)),
            pl.BlockSpec((None, conv_w.shape[1], bf), lambda i, f: (layer, 0, f)),
            pl.BlockSpec((None, 1, bf), lambda i, f: (layer, 0, f)),
            pl.BlockSpec((None, 2 * SUB, bf), lambda i, f: (i, 0, f)),
        ],
        out_specs=pl.BlockSpec((bm, bf), lambda i, f: (i, f)),
        out_shape=jax.ShapeDtypeStruct((m, d_ff), BF16),
        scratch_shapes=[pltpu.VMEM((bm + 2 * SUB, bf), F32)],
        compiler_params=_params("parallel", "parallel"),
        name="gate_up",
    )(h, wg, wu, conv_w, conv_b, edge)


def _rope_tables(n_tokens):
    axis_dim = LANE // 2
    half = axis_dim // 2
    rows = n_tokens // GRID_W
    row = jnp.repeat(jnp.arange(rows, dtype=jnp.int32), GRID_W)
    col = jnp.tile(jnp.arange(GRID_W, dtype=jnp.int32), rows)
    inv_freq = ROPE_THETA ** (-jnp.arange(half, dtype=F32) / half)
    cos_parts, sin_parts = [], []
    for pos in (row, col):
        ang = pos.astype(F32)[:, None] * inv_freq[None, :]
        c, s = jnp.cos(ang), jnp.sin(ang)
        cos_parts += [c, c]
        sin_parts += [-s, s]
    return jnp.concatenate(cos_parts, axis=1), jnp.concatenate(sin_parts, axis=1)


def kernel(x_prompt, x_sample, cache_k, cache_v, c, c_ctx, w_mod, b_mod, norm_attn, norm_ffn, w_qkv, w_o,
           sink_a, q_norm_b, k_norm_b, w_gate, w_up, w_down, conv_w, conv_b, norm_f):
    batch_p, seq_p, d = x_prompt.shape
    batch_s, seq_s, _ = x_sample.shape
    depth = w_mod.shape[0]
    n_kv, head_dim = cache_k.shape[3], cache_k.shape[4]
    n_heads = sink_a.shape[1]
    assert head_dim == LANE
    q_dim, kv_dim = n_heads * head_dim, n_kv * head_dim
    d_ff = w_gate.shape[2]
    scale = head_dim ** -0.5
    m_p, m_s = batch_p * seq_p, batch_s * seq_s

    ffn_m_p = _tile(m_p, TILES["ffn_m"], seq_p)
    ffn_m_s = _tile(seq_s, TILES["ffn_m"], 16)

    xp = x_prompt.reshape(m_p, d)
    xs = x_sample.reshape(m_s, d)

    n_cond = -(-(batch_s + 1) // SUB) * SUB
    cond = jnp.zeros((n_cond, d), F32).at[:batch_s].set(c).at[batch_s].set(c_ctx)
    mod_table = _modulation(cond, w_mod, b_mod).reshape(depth * n_cond, 1, -1)

    wqkv = w_qkv.astype(BF16)
    wo = w_o.astype(BF16)
    wg = w_gate.astype(BF16)
    wu = w_up.astype(BF16)
    wd = w_down.astype(BF16)
    cw = conv_w
    cb = conv_b.reshape(depth, 1, d_ff)
    gains_attn = norm_attn.reshape(depth, 1, d)
    gains_ffn = norm_ffn.reshape(depth, 1, d)
    k_ctx = cache_k.transpose(0, 1, 3, 2, 4).astype(BF16)
    v_ctx_t = cache_v.transpose(0, 1, 3, 4, 2).astype(BF16)

    rope_tables = _rope_tables(seq_s)
    ones_kv = jnp.ones((kv_dim,), F32)
    ks_out, vs_out = [], []
    for i in range(depth):
        mod_s = _Mod(mod_table, i * n_cond, seq_s)
        mod_p = _Mod(mod_table, i * n_cond + batch_s, m_p)
        mixer_a = i % 2 == 0
        if mixer_a:
            sink = sink_a[i // 2].astype(F32) * LOG2E
            colscale = jnp.concatenate([jnp.full((q_dim,), scale * LOG2E, F32), ones_kv, ones_kv])
        else:
            sink = jnp.zeros((n_heads,), F32)
            colscale = jnp.concatenate([jnp.tile(q_norm_b[i // 2].astype(F32), n_heads) * (scale * LOG2E),
                                        jnp.tile(k_norm_b[i // 2].astype(F32), n_kv), ones_kv])
        colscale = colscale.reshape(1, -1)

        hp = _norm_mod(xp, gains_attn, i, mod_p, 0)
        hs = _norm_mod(xs, gains_attn, i, mod_s, 0)
        qkv_p, kv32 = _qkv_proj(hp, wqkv, i, colscale, q_dim, kv_dim, norm=not mixer_a, rope_tables=None,
                                want_kv32=True)
        qkv_s, = _qkv_proj(hs, wqkv, i, colscale, q_dim, kv_dim, norm=not mixer_a, rope_tables=rope_tables,
                           want_kv32=False)
        ks_out.append(kv32[:, :kv_dim].reshape(batch_p, seq_p, n_kv, head_dim))
        vs_out.append(kv32[:, kv_dim:].reshape(batch_p, seq_p, n_kv, head_dim))
        op = _self_attention(qkv_p, qkv_p[:, q_dim + kv_dim:].T, sink, batch_p, q_dim, kv_dim,
                             has_sink=mixer_a)
        v_s = qkv_s[:, q_dim + kv_dim:]
        if mixer_a:
            v_t = v_s.reshape(m_s // LANE, LANE, n_kv, LANE).transpose(2, 0, 3, 1)
            osm = _window_attention(qkv_s, v_t, k_ctx, v_ctx_t, i, sink, batch_s, q_dim, kv_dim)
        else:
            osm = _dense_attention(qkv_s, v_s.T, k_ctx, v_ctx_t, i, batch_s, q_dim, kv_dim)
        xp = _proj_residual(op, wo, i, xp, mod_p, 2, TILES["proj_m"], TILES["proj_n"])
        xs = _proj_residual(osm, wo, i, xs, mod_s, 2, TILES["proj_m"], TILES["proj_n"])

        hp = _norm_mod(xp, gains_ffn, i, mod_p, 3)
        hs = _norm_mod(xs, gains_ffn, i, mod_s, 3)
        zp = _gate_up(hp, wg, wu, cw, cb, i, seq_p, ffn_m_p)
        zs = _gate_up(hs, wg, wu, cw, cb, i, seq_s, ffn_m_s)
        xp = _proj_residual(zp, wd, i, xp, mod_p, 5, TILES["down_m"], TILES["down_n"])
        xs = _proj_residual(zs, wd, i, xs, mod_s, 5, TILES["down_m"], TILES["down_n"])

    y_prompt = _final_norm(xp, norm_f).reshape(batch_p, seq_p, d)
    y_sample = _final_norm(xs, norm_f).reshape(batch_s, seq_s, d)
    return y_prompt, y_sample, jnp.stack(ks_out, axis=1), jnp.stack(vs_out, axis=1)
```

```python
import functools
import math

import jax
import jax.numpy as jnp
from jax import lax
from jax.experimental import pallas as pl
from jax.experimental.pallas import tpu as pltpu

GRID_W = 64
WINDOW = 128
ROPE_THETA = 10000.0
EPS = 1e-6
LANE = 128
SUB = 8
PACK = 16
NORM_ROWS = PACK
NORM_UNROLL = 8
NEG_BIG = -1e30
UNDERFLOW_SUM = 2.0 ** -60
RETRY_SHIFT = 64.0
LOG2E = math.log2(math.e)
VMEM_LIMIT_BYTES = 58 * 1024 * 1024

BF16 = jnp.bfloat16
F32 = jnp.float32

TILES = dict(
    mod_n=512,
    norm_m=512,
    qkv_m=1024, qkv_n=1024, qkv_rope_m=512,
    proj_m=1024, proj_n=1024,
    ffn_m=1024, ffn_f=512,
    down_m=512, down_n=512,
    win_q=256,
    attn_q=256, attn_kv=2048,
)


def _tile(dim, pref, mult):
    if dim <= pref:
        return dim
    t = pref - pref % mult
    while t >= mult:
        if dim % t == 0:
            return t
        t -= mult
    return dim


def _params(*semantics):
    return pltpu.CompilerParams(dimension_semantics=semantics, vmem_limit_bytes=VMEM_LIMIT_BYTES)


def _silu(x):
    return x / (1.0 + jnp.exp(-x))


def _dot(a, b):
    return jnp.dot(a, b, preferred_element_type=F32)


def _dot_nt(a, b):
    return lax.dot_general(a, b, (((1,), (1,)), ((), ())), preferred_element_type=F32)


def _mod_kernel(c_ref, w_ref, b_ref, o_ref):
    a = _silu(c_ref[...]).astype(BF16)
    o_ref[...] = _dot(a, w_ref[...].astype(BF16)) + b_ref[...]


def _modulation(cond, w_mod, b_mod):
    depth, d, n = w_mod.shape
    rows = cond.shape[0]
    tn = _tile(n, TILES["mod_n"], LANE)
    return pl.pallas_call(
        _mod_kernel,
        grid=(depth, n // tn),
        in_specs=[
            pl.BlockSpec((rows, d), lambda l, j: (0, 0)),
            pl.BlockSpec((None, d, tn), lambda l, j: (l, 0, j)),
            pl.BlockSpec((None, 1, tn), lambda l, j: (l, 0, j)),
        ],
        out_specs=pl.BlockSpec((None, rows, tn), lambda l, j: (l, 0, j)),
        out_shape=jax.ShapeDtypeStruct((depth, rows, n), F32),
        compiler_params=_params("parallel", "parallel"),
        name="modulation",
    )(cond, w_mod, b_mod.reshape(depth, 1, n))


class _Mod:
    def __init__(self, table, base, rows_per_group):
        self.table, self.base, self.rows_per_group = table, base, rows_per_group

    def spec(self, slot, bm, bn, nj):
        base, rpg = self.base, self.rows_per_group
        if nj == 1:
            return pl.BlockSpec((None, 1, bn), lambda i, *_: (base + (i * bm) // rpg, 0, slot))
        return pl.BlockSpec((None, 1, bn), lambda i, j: (base + (i * bm) // rpg, 0, slot * nj + j))


def _rms_rows(x_ref, o_ref, scale_ref, shift_ref):
    def body(r, carry):
        rows = pl.ds(pl.multiple_of(r * NORM_ROWS, NORM_ROWS), NORM_ROWS)
        x = x_ref[rows, :]
        inv = lax.rsqrt(jnp.mean(x * x, axis=-1, keepdims=True) + EPS)
        y = x_ref[rows, :] * inv * scale_ref[...]
        if shift_ref is not None:
            y = y + shift_ref[...]
        o_ref[rows, :] = y.astype(o_ref.dtype)
        return carry

    lax.fori_loop(0, x_ref.shape[0] // NORM_ROWS, body, 0, unroll=NORM_UNROLL)


def _norm_mod_kernel(x_ref, g_ref, sh_ref, sc_ref, o_ref, gs_ref):
    gs_ref[...] = g_ref[...] * (1.0 + sc_ref[...])
    _rms_rows(x_ref, o_ref, gs_ref, sh_ref)


def _norm_mod(x, gains, layer, mod, shift_slot):
    m, d = x.shape
    bm = _tile(min(m, mod.rows_per_group), TILES["norm_m"], NORM_ROWS)
    return pl.pallas_call(
        _norm_mod_kernel,
        grid=(m // bm,),
        in_specs=[
            pl.BlockSpec((bm, d), lambda i: (i, 0)),
            pl.BlockSpec((None, 1, d), lambda i: (layer, 0, 0)),
            mod.spec(shift_slot, bm, d, 1),
            mod.spec(shift_slot + 1, bm, d, 1),
        ],
        out_specs=pl.BlockSpec((bm, d), lambda i: (i, 0)),
        out_shape=jax.ShapeDtypeStruct((m, d), BF16),
        scratch_shapes=[pltpu.VMEM((1, d), F32)],
        compiler_params=_params("parallel"),
        name="norm_mod",
    )(x, gains, mod.table, mod.table)


def _norm_kernel(x_ref, g_ref, o_ref):
    _rms_rows(x_ref, o_ref, g_ref, None)


def _final_norm(x, gain):
    m, d = x.shape
    bm = _tile(m, TILES["norm_m"], NORM_ROWS)
    return pl.pallas_call(
        _norm_kernel,
        grid=(m // bm,),
        in_specs=[pl.BlockSpec((bm, d), lambda i: (i, 0)), pl.BlockSpec((1, d), lambda i: (0, 0))],
        out_specs=pl.BlockSpec((bm, d), lambda i: (i, 0)),
        out_shape=jax.ShapeDtypeStruct((m, d), F32),
        compiler_params=_params("parallel"),
        name="final_norm",
    )(x, gain.reshape(1, d))


def _qkv_kernel(h_ref, w_ref, cs_ref, o_ref, kv_ref, *, n_qk, norm):
    is_v = pl.program_id(1) >= n_qk
    acc = _dot(h_ref[...], w_ref[...])
    outs = []
    for hh in range(acc.shape[1] // LANE):
        blk = acc[:, hh * LANE:(hh + 1) * LANE]
        if norm:
            inv = lax.rsqrt(jnp.mean(blk * blk, axis=-1, keepdims=True) + EPS)
            blk = blk * jnp.where(is_v, 1.0, inv)
        outs.append(blk * cs_ref[:, hh * LANE:(hh + 1) * LANE])
    res = jnp.concatenate(outs, axis=1) if len(outs) > 1 else outs[0]
    o_ref[...] = res.astype(o_ref.dtype)
    kv_ref[...] = res


def _qkv_proj(h, w, layer, colscale, q_dim, kv_dim, *, norm):
    m, d = h.shape
    n = w.shape[2]
    bm = _tile(m, TILES["qkv_m"], PACK)
    bn = _tile(kv_dim, TILES["qkv_n"], LANE)
    assert q_dim % bn == 0 and kv_dim % bn == 0
    n_q = q_dim // bn
    return pl.pallas_call(
        functools.partial(_qkv_kernel, n_qk=(q_dim + kv_dim) // bn, norm=norm),
        grid=(m // bm, n // bn),
        in_specs=[
            pl.BlockSpec((bm, d), lambda i, j: (i, 0)),
            pl.BlockSpec((None, d, bn), lambda i, j: (layer, 0, j)),
            pl.BlockSpec((1, bn), lambda i, j: (0, j)),
        ],
        out_specs=[pl.BlockSpec((bm, bn), lambda i, j: (i, j)),
                   pl.BlockSpec((bm, bn), lambda i, j: (i, jnp.maximum(j - n_q, 0)))],
        out_shape=[jax.ShapeDtypeStruct((m, n), BF16), jax.ShapeDtypeStruct((m, 2 * kv_dim), F32)],
        compiler_params=_params("parallel", "arbitrary"),
        name="qkv_proj",
    )(h, w, colscale)


def _head_epilogue(acc, cs_ref, cos, sin, is_v, norm):
    lane = lax.broadcasted_iota(jnp.int32, (1, LANE), 1)
    first_half = (lane % (LANE // 2)) < (LANE // 4)
    outs = []
    for hh in range(acc.shape[1] // LANE):
        blk = acc[:, hh * LANE:(hh + 1) * LANE]
        if norm:
            inv = lax.rsqrt(jnp.mean(blk * blk, axis=-1, keepdims=True) + EPS)
            blk = blk * jnp.where(is_v, 1.0, inv)
        blk = blk * cs_ref[:, hh * LANE:(hh + 1) * LANE]
        partner = jnp.where(first_half, pltpu.roll(blk, LANE - LANE // 4, 1),
                            pltpu.roll(blk, LANE // 4, 1))
        outs.append(blk * cos + partner * sin)
    return jnp.concatenate(outs, axis=1) if len(outs) > 1 else outs[0]


def _qkv_lag_kernel(h_ref, w_ref, cs_ref, cos_ref, sin_ref, o_ref, acc_ref, *, n_i, n_qk, norm):
    s = pl.program_id(0)

    @pl.when(s == 0)
    def _():
        acc_ref[...] = jnp.zeros_like(acc_ref)

    is_v = jnp.maximum(s - 1, 0) // n_i >= n_qk
    cos = jnp.where(is_v, 1.0, cos_ref[...])
    sin = jnp.where(is_v, 0.0, sin_ref[...])
    o_ref[...] = _head_epilogue(acc_ref[...], cs_ref, cos, sin, is_v, norm).astype(o_ref.dtype)
    acc_ref[...] = _dot(h_ref[...], w_ref[...])


def _qkv_proj_rope(h, w, layer, colscale, q_dim, kv_dim, *, norm, rope_tables):
    m, d = h.shape
    n = w.shape[2]
    bm = _tile(m, TILES["qkv_rope_m"], PACK)
    bn = _tile(kv_dim, TILES["qkv_n"], LANE)
    assert q_dim % bn == 0 and kv_dim % bn == 0
    n_i = m // bm
    steps = n_i * (n // bn)
    cos, sin = rope_tables
    per = cos.shape[0] // bm
    assert cos.shape[0] % bm == 0
    cur = lambda s: jnp.minimum(s, steps - 1)
    fin = lambda s: jnp.maximum(s - 1, 0)
    return pl.pallas_call(
        functools.partial(_qkv_lag_kernel, n_i=n_i, n_qk=(q_dim + kv_dim) // bn, norm=norm),
        grid=(steps + 1,),
        in_specs=[
            pl.BlockSpec((bm, d), lambda s: (cur(s) % n_i, 0)),
            pl.BlockSpec((None, d, bn), lambda s: (layer, 0, cur(s) // n_i)),
            pl.BlockSpec((1, bn), lambda s: (0, fin(s) // n_i)),
            pl.BlockSpec((bm, LANE), lambda s: ((fin(s) % n_i) % per, 0)),
            pl.BlockSpec((bm, LANE), lambda s: ((fin(s) % n_i) % per, 0)),
        ],
        out_specs=pl.BlockSpec((bm, bn), lambda s: (fin(s) % n_i, fin(s) // n_i)),
        out_shape=jax.ShapeDtypeStruct((m, n), BF16),
        scratch_shapes=[pltpu.VMEM((bm, bn), F32)],
        compiler_params=_params("arbitrary"),
        name="qkv_proj_rope",
    )(h, w, colscale, cos, sin)


def _stack_heads(q_ref, r_heads):
    return jnp.concatenate([q_ref[:, r * LANE:(r + 1) * LANE] for r in range(r_heads)], axis=0)


def _store_heads_t(o_ref, o_t, r_heads):
    tq = o_t.shape[1] // r_heads
    for r in range(r_heads):
        o_ref[:, r * LANE:(r + 1) * LANE] = o_t[:, r * tq:(r + 1) * tq].T.astype(o_ref.dtype)


def _max_key_norm(*k_refs):
    def max_sq_norm(ref):
        x = ref[...].astype(F32)
        return jnp.max(jnp.sum(x * x, axis=1, keepdims=True))
    return jnp.sqrt(functools.reduce(jnp.maximum, [max_sq_norm(r) for r in k_refs]))


def _sink_row(sink_ref, g, r_heads, tq):
    return jnp.concatenate(
        [jnp.full((1, tq), sink_ref[g * r_heads + r], F32) for r in range(r_heads)], axis=1)


def _attend(q_ref, o_ref, shift_ref, acc_ref, segments, k_max, sink, r_heads):
    qs = _stack_heads(q_ref, r_heads)
    rows = qs.shape[0]
    q_sq = _dot_nt(jnp.ones((SUB, LANE), BF16), qs * qs)[0:1]
    bound = jnp.sqrt(q_sq) * k_max
    if sink is not None:
        bound = jnp.maximum(bound, sink)

    def attempt(off):
        den = jnp.zeros((1, rows), F32)
        acc = jnp.zeros((LANE, rows), F32)
        for k, v_t, bias in segments():
            s = _dot_nt(k, qs)
            if bias is not None:
                s = s + bias
            p = jnp.exp2(s - off)
            den = den + jnp.sum(p, axis=0, keepdims=True)
            acc = acc + _dot(v_t, p.astype(BF16))
        if sink is not None:
            den = den + jnp.exp2(sink - off)
        return acc * (1.0 / den), den

    out, den = attempt(bound)
    _store_heads_t(o_ref, out, r_heads)

    @pl.when(jnp.min(den) < UNDERFLOW_SUM)
    def _():
        shift_ref[...] = jnp.where(den < UNDERFLOW_SUM, RETRY_SHIFT, 0.0)

        def retry(_):
            shift = shift_ref[...]
            out, den = attempt(bound - shift)
            acc_ref[...] = out
            shift_ref[...] = jnp.where(den < UNDERFLOW_SUM, shift + RETRY_SHIFT, shift)
            return jnp.min(den)

        lax.while_loop(lambda min_sum: min_sum < UNDERFLOW_SUM, retry, jnp.float32(0.0))
        _store_heads_t(o_ref, acc_ref[...], r_heads)


def _attn_scratch(rows):
    return [pltpu.SMEM((1,), F32), pltpu.VMEM((1, rows), F32), pltpu.VMEM((LANE, rows), F32)]


def _self_attn_kernel(sink_ref, q_ref, k_ref, vt_ref, o_ref, kmax_ref, shift_ref, acc_ref, *,
                      r_heads, has_sink):
    kmax_ref[0] = _max_key_norm(k_ref)
    sink = _sink_row(sink_ref, pl.program_id(1), r_heads, q_ref.shape[0]) if has_sink else None
    _attend(q_ref, o_ref, shift_ref, acc_ref, lambda: [(k_ref[...], vt_ref[...], None)],
            kmax_ref[0], sink, r_heads)


def _self_attention(qkv, v_t, sink, batch, q_dim, kv_dim, *, has_sink):
    m = qkv.shape[0]
    t = m // batch
    n_kv = kv_dim // LANE
    r_heads = q_dim // kv_dim
    qw = r_heads * LANE
    return pl.pallas_call(
        functools.partial(_self_attn_kernel, r_heads=r_heads, has_sink=has_sink),
        grid=(batch, n_kv),
        in_specs=[
            pl.BlockSpec(memory_space=pltpu.SMEM),
            pl.BlockSpec((t, qw), lambda b, g: (b, g)),
            pl.BlockSpec((t, LANE), lambda b, g: (b, q_dim // LANE + g)),
            pl.BlockSpec((LANE, t), lambda b, g: (g, b)),
        ],
        out_specs=pl.BlockSpec((t, qw), lambda b, g: (b, g)),
        out_shape=jax.ShapeDtypeStruct((m, q_dim), BF16),
        scratch_shapes=_attn_scratch(r_heads * t),
        compiler_params=_params("parallel", "parallel"),
        name="self_attention",
    )(sink, qkv, qkv, v_t)


def _win_attn_kernel(sink_ref, q_ref, k_ref, vt_ref, kc_ref, vct_ref, bias_ref, o_ref,
                     kmax_ref, shift_ref, acc_ref, *, r_heads, span):
    i = pl.program_id(2)
    tq = q_ref.shape[0]
    t = k_ref.shape[0]

    @pl.when(i == 0)
    def _():
        kmax_ref[0] = _max_key_norm(k_ref, kc_ref)

    base = pl.multiple_of(jnp.clip(i * tq - WINDOW, 0, t - span), LANE)

    def segments():
        k_w = k_ref[pl.ds(base, span), :]
        v_w_t = jnp.concatenate([vt_ref[base // LANE + c] for c in range(span // LANE)], axis=1)
        bias = jnp.concatenate([bias_ref[...]] * r_heads, axis=1)
        return [(kc_ref[...], vct_ref[...], None), (k_w, v_w_t, bias)]

    _attend(q_ref, o_ref, shift_ref, acc_ref, segments, kmax_ref[0],
            _sink_row(sink_ref, pl.program_id(1), r_heads, tq), r_heads)


def _window_attention(qkv, v_t, k_ctx, v_ctx_t, layer, sink, batch, q_dim, kv_dim):
    m = qkv.shape[0]
    t = m // batch
    n_kv = kv_dim // LANE
    r_heads = q_dim // kv_dim
    qw = r_heads * LANE
    tq = _tile(t, TILES["win_q"], LANE)
    reach = -(-WINDOW // LANE) * LANE
    span = tq + 2 * reach
    assert t % tq == 0 and t >= span
    nb = t // tq
    p = k_ctx.shape[3]
    blk = jnp.array([0, min(1, nb - 1), nb - 1], dtype=jnp.int32)[:, None, None]
    base = jnp.clip(blk * tq - reach, 0, t - span)
    kpos = base + jnp.arange(span, dtype=jnp.int32)[None, :, None]
    qpos = blk * tq + jnp.arange(tq, dtype=jnp.int32)[None, None, :]
    bias = jnp.where(jnp.abs(qpos - kpos) <= WINDOW, 0.0, NEG_BIG).astype(F32)
    variant = lambda i: jnp.where(i == 0, 0, jnp.where(i == nb - 1, 2, 1))
    return pl.pallas_call(
        functools.partial(_win_attn_kernel, r_heads=r_heads, span=span),
        grid=(batch, n_kv, nb),
        in_specs=[
            pl.BlockSpec(memory_space=pltpu.SMEM),
            pl.BlockSpec((tq, qw), lambda b, g, i: (b * nb + i, g)),
            pl.BlockSpec((t, LANE), lambda b, g, i: (b, q_dim // LANE + g)),
            pl.BlockSpec((None, t // LANE, LANE, LANE), lambda b, g, i: (g, b, 0, 0)),
            pl.BlockSpec((None, None, None, p, LANE), lambda b, g, i: (b, layer, g, 0, 0)),
            pl.BlockSpec((None, None, None, LANE, p), lambda b, g, i: (b, layer, g, 0, 0)),
            pl.BlockSpec((None, span, tq), lambda b, g, i: (variant(i), 0, 0)),
        ],
        out_specs=pl.BlockSpec((tq, qw), lambda b, g, i: (b * nb + i, g)),
        out_shape=jax.ShapeDtypeStruct((m, q_dim), BF16),
        scratch_shapes=_attn_scratch(r_heads * tq),
        compiler_params=_params("parallel", "parallel", "arbitrary"),
        name="window_attention",
    )(sink, qkv, qkv, v_t, k_ctx, v_ctx_t, bias)


def _dense_attn_kernel(q_ref, k_ref, vt_ref, kc_ref, vct_ref, o_ref, kmax_ref, shift_ref, acc_ref, *,
                       r_heads, kv_chunk):
    t = k_ref.shape[0]

    @pl.when(pl.program_id(2) == 0)
    def _():
        kmax_ref[0] = _max_key_norm(k_ref, kc_ref)

    def segments():
        return [(kc_ref[...], vct_ref[...], None)] + [
            (k_ref[c * kv_chunk:(c + 1) * kv_chunk, :], vt_ref[:, c * kv_chunk:(c + 1) * kv_chunk], None)
            for c in range(t // kv_chunk)]

    _attend(q_ref, o_ref, shift_ref, acc_ref, segments, kmax_ref[0], None, r_heads)


def _dense_attention(qkv, v_t, k_ctx, v_ctx_t, layer, batch, q_dim, kv_dim):
    m = qkv.shape[0]
    t = m // batch
    n_kv = kv_dim // LANE
    r_heads = q_dim // kv_dim
    qw = r_heads * LANE
    tq = _tile(t, TILES["attn_q"], LANE)
    kv_chunk = _tile(t, TILES["attn_kv"], LANE)
    nb = t // tq
    p = k_ctx.shape[3]
    return pl.pallas_call(
        functools.partial(_dense_attn_kernel, r_heads=r_heads, kv_chunk=kv_chunk),
        grid=(batch, n_kv, nb),
        in_specs=[
            pl.BlockSpec((tq, qw), lambda b, g, i: (b * nb + i, g)),
            pl.BlockSpec((t, LANE), lambda b, g, i: (b, q_dim // LANE + g)),
            pl.BlockSpec((LANE, t), lambda b, g, i: (g, b)),
            pl.BlockSpec((None, None, None, p, LANE), lambda b, g, i: (b, layer, g, 0, 0)),
            pl.BlockSpec((None, None, None, LANE, p), lambda b, g, i: (b, layer, g, 0, 0)),
        ],
        out_specs=pl.BlockSpec((tq, qw), lambda b, g, i: (b * nb + i, g)),
        out_shape=jax.ShapeDtypeStruct((m, q_dim), BF16),
        scratch_shapes=_attn_scratch(r_heads * tq),
        compiler_params=_params("parallel", "parallel", "arbitrary"),
        name="dense_attention",
    )(qkv, qkv, v_t, k_ctx, v_ctx_t)


def _proj_res_kernel(a_ref, w_ref, x_ref, g_ref, o_ref):
    o_ref[...] = x_ref[...] + g_ref[...] * _dot(a_ref[...], w_ref[...])


def _proj_residual(a, w, layer, x, mod, gate_slot, bm_pref, bn_pref):
    m, k = a.shape
    d = w.shape[2]
    bm = _tile(min(m, mod.rows_per_group), bm_pref, PACK)
    bn = _tile(d, bn_pref, LANE)
    nj = d // bn
    gate = mod.spec(gate_slot, bm, bn, nj)
    return pl.pallas_call(
        _proj_res_kernel,
        grid=(nj, m // bm),
        in_specs=[
            pl.BlockSpec((bm, k), lambda j, i: (i, 0)),
            pl.BlockSpec((None, k, bn), lambda j, i: (layer, 0, j)),
            pl.BlockSpec((bm, bn), lambda j, i: (i, j)),
            pl.BlockSpec(gate.block_shape, lambda j, i: gate.index_map(i, j)),
        ],
        out_specs=pl.BlockSpec((bm, bn), lambda j, i: (i, j)),
        out_shape=jax.ShapeDtypeStruct((m, d), F32),
        compiler_params=_params("parallel", "parallel"),
        name="proj_residual",
    )(a, w, x, mod.table)


def _matmul_kernel(a_ref, w_ref, o_ref):
    o_ref[...] = _dot(a_ref[...], w_ref[...])


def _edge_rows_kernel(before_ref, after_ref, o_ref):
    row = lax.broadcasted_iota(jnp.int32, o_ref.shape, 0)
    last = before_ref[...].astype(F32)[PACK - 1:PACK, :]
    first = after_ref[...].astype(F32)[0:1, :]
    o_ref[...] = jnp.where(row == SUB - 1, last, jnp.where(row == SUB, first, 0.0)).astype(o_ref.dtype)


def _edge_gate(h, wg, layer, bm):
    m, d = h.shape
    nt = m // bm
    d_ff = wg.shape[2]
    per = bm // PACK
    edge = pl.pallas_call(
        _edge_rows_kernel,
        grid=(nt,),
        in_specs=[pl.BlockSpec((PACK, d), lambda i: (jnp.maximum(i * per - 1, 0), 0)),
                  pl.BlockSpec((PACK, d), lambda i: (jnp.minimum((i + 1) * per, nt * per - 1), 0))],
        out_specs=pl.BlockSpec((None, PACK, d), lambda i: (i, 0, 0)),
        out_shape=jax.ShapeDtypeStruct((nt, PACK, d), h.dtype),
        compiler_params=_params("parallel"),
        name="edge_rows",
    )(h, h)
    bn = min(d_ff, TILES["ffn_f"])
    out = pl.pallas_call(
        _matmul_kernel,
        grid=(pl.cdiv(d_ff, bn),),
        in_specs=[pl.BlockSpec((nt * PACK, d), lambda j: (0, 0)),
                  pl.BlockSpec((None, d, bn), lambda j: (layer, 0, j))],
        out_specs=pl.BlockSpec((nt * PACK, bn), lambda j: (0, j)),
        out_shape=jax.ShapeDtypeStruct((nt * PACK, d_ff), F32),
        compiler_params=_params("parallel"),
        name="edge_gate",
    )(edge.reshape(nt * PACK, d), wg)
    return out.reshape(nt, PACK, d_ff)


def _gate_up_kernel(*refs, bm, seq_len, n_cast):
    h_ref, wg_ref, wu_ref, cw_ref, cb_ref, e_ref = refs[:6]
    cast_in = refs[6:6 + n_cast]
    o_ref = refs[6 + n_cast]
    cast_out = refs[7 + n_cast:7 + 2 * n_cast]
    a_ref = refs[7 + 2 * n_cast]
    for src, dst in zip(cast_in, cast_out):
        dst[...] = src[...].astype(dst.dtype)
    a_ref[0:SUB, :] = e_ref[0:SUB, :]
    a_ref[SUB:SUB + bm, :] = _dot(h_ref[...], wg_ref[...])
    a_ref[SUB + bm:, :] = e_ref[SUB:, :]
    u = _dot(h_ref[...], wu_ref[...])
    row = pl.program_id(0) * bm + lax.broadcasted_iota(jnp.int32, (bm, 1), 0)
    pos = row % seq_len
    prev = jnp.where(pos == 0, 0.0, a_ref[SUB - 1:SUB - 1 + bm, :])
    nxt = jnp.where(pos == seq_len - 1, 0.0, a_ref[SUB + 1:SUB + 1 + bm, :])
    cur = a_ref[SUB:SUB + bm, :]
    a = prev * cw_ref[0:1, :] + cur * cw_ref[1:2, :] + nxt * cw_ref[2:3, :] + cb_ref[...]
    o_ref[...] = (_silu(a) * u).astype(o_ref.dtype)


def _gate_up(h, wg, wu, conv_w, conv_b, layer, seq_len, bm, casts=()):
    m, d = h.shape
    (wg_arr, wg_l), (wu_arr, wu_l) = wg, wu
    d_ff = wg_arr.shape[2]
    bf = min(d_ff, TILES["ffn_f"])
    n_f = pl.cdiv(d_ff, bf)
    steps = (m // bm) * n_f
    edge = _edge_gate(h, wg_arr, wg_l, bm)
    cast_in_specs, cast_out_specs, cast_shapes = [], [], []
    for arr, l in casts:
        _, r, c = arr.shape
        rows = PACK * pl.cdiv(r, PACK * steps)
        blk = lambda i, f, n=pl.cdiv(r, rows): jnp.minimum(i * n_f + f, n - 1)
        cast_in_specs.append(pl.BlockSpec((None, rows, c), lambda i, f, l=l, blk=blk: (l, blk(i, f), 0)))
        cast_out_specs.append(pl.BlockSpec((rows, c), lambda i, f, blk=blk: (blk(i, f), 0)))
        cast_shapes.append(jax.ShapeDtypeStruct((r, c), BF16))
    out = pl.pallas_call(
        functools.partial(_gate_up_kernel, bm=bm, seq_len=seq_len, n_cast=len(casts)),
        grid=(m // bm, n_f),
        in_specs=[
            pl.BlockSpec((bm, d), lambda i, f: (i, 0)),
            pl.BlockSpec((None, d, bf), lambda i, f: (wg_l, 0, f)),
            pl.BlockSpec((None, d, bf), lambda i, f: (wu_l, 0, f)),
            pl.BlockSpec((None, conv_w.shape[1], bf), lambda i, f: (layer, 0, f)),
            pl.BlockSpec((None, 1, bf), lambda i, f: (layer, 0, f)),
            pl.BlockSpec((None, PACK, bf), lambda i, f: (i, 0, f)),
        ] + cast_in_specs,
        out_specs=[pl.BlockSpec((bm, bf), lambda i, f: (i, f))] + cast_out_specs,
        out_shape=[jax.ShapeDtypeStruct((m, d_ff), BF16)] + cast_shapes,
        scratch_shapes=[pltpu.VMEM((bm + PACK, bf), F32)],
        compiler_params=_params("arbitrary", "arbitrary"),
        name="gate_up",
    )(h, wg_arr, wu_arr, conv_w, conv_b, edge, *[arr for arr, _ in casts])
    return out[0], list(out[1:])


def _rope_tables(n_tokens):
    axis_dim = LANE // 2
    half = axis_dim // 2
    rows = n_tokens // GRID_W
    row = jnp.repeat(jnp.arange(rows, dtype=jnp.int32), GRID_W)
    col = jnp.tile(jnp.arange(GRID_W, dtype=jnp.int32), rows)
    inv_freq = ROPE_THETA ** (-jnp.arange(half, dtype=F32) / half)
    cos_parts, sin_parts = [], []
    for pos in (row, col):
        ang = pos.astype(F32)[:, None] * inv_freq[None, :]
        c, s = jnp.cos(ang), jnp.sin(ang)
        cos_parts += [c, c]
        sin_parts += [-s, s]
    return jnp.concatenate(cos_parts, axis=1), jnp.concatenate(sin_parts, axis=1)


def kernel(x_prompt, x_sample, cache_k, cache_v, c, c_ctx, w_mod, b_mod, norm_attn, norm_ffn, w_qkv, w_o,
           sink_a, q_norm_b, k_norm_b, w_gate, w_up, w_down, conv_w, conv_b, norm_f):
    batch_p, seq_p, d = x_prompt.shape
    batch_s, seq_s, _ = x_sample.shape
    depth = w_mod.shape[0]
    n_kv, head_dim = cache_k.shape[3], cache_k.shape[4]
    n_heads = sink_a.shape[1]
    assert head_dim == LANE
    q_dim, kv_dim = n_heads * head_dim, n_kv * head_dim
    d_ff = w_gate.shape[2]
    scale = head_dim ** -0.5
    m_p, m_s = batch_p * seq_p, batch_s * seq_s

    ffn_m_p = _tile(m_p, TILES["ffn_m"], seq_p)
    ffn_m_s = _tile(seq_s, TILES["ffn_m"], PACK)

    xp = x_prompt.reshape(m_p, d)
    xs = x_sample.reshape(m_s, d)

    n_cond = -(-(batch_s + 1) // SUB) * SUB
    cond = jnp.zeros((n_cond, d), F32).at[:batch_s].set(c).at[batch_s].set(c_ctx)
    mod_table = _modulation(cond, w_mod, b_mod).reshape(depth * n_cond, 1, -1)

    f32_weights = dict(qkv=w_qkv, o=w_o, gate=w_gate, up=w_up, down=w_down)
    bf16 = {(name, 0): (f32_weights[name][:1].astype(BF16), 0) for name in ("qkv", "o", "gate", "up")}
    cw = conv_w
    cb = conv_b.reshape(depth, 1, d_ff)
    gains_attn = norm_attn.reshape(depth, 1, d)
    gains_ffn = norm_ffn.reshape(depth, 1, d)
    k_ctx = cache_k.transpose(0, 1, 3, 2, 4).astype(BF16)
    v_ctx_t = cache_v.transpose(0, 1, 3, 4, 2).astype(BF16)

    rope_tables = _rope_tables(seq_s)
    ones_kv = jnp.ones((kv_dim,), F32)
    ks_out, vs_out = [], []
    for i in range(depth):
        mod_s = _Mod(mod_table, i * n_cond, seq_s)
        mod_p = _Mod(mod_table, i * n_cond + batch_s, m_p)
        mixer_a = i % 2 == 0
        if mixer_a:
            sink = sink_a[i // 2].astype(F32) * LOG2E
            colscale = jnp.concatenate([jnp.full((q_dim,), scale * LOG2E, F32), ones_kv, ones_kv])
        else:
            sink = jnp.zeros((n_heads,), F32)
            colscale = jnp.concatenate([jnp.tile(q_norm_b[i // 2].astype(F32), n_heads) * (scale * LOG2E),
                                        jnp.tile(k_norm_b[i // 2].astype(F32), n_kv), ones_kv])
        colscale = colscale.reshape(1, -1)

        hp = _norm_mod(xp, gains_attn, i, mod_p, 0)
        hs = _norm_mod(xs, gains_attn, i, mod_s, 0)
        qkv_p, kv32 = _qkv_proj(hp, *bf16["qkv", i], colscale, q_dim, kv_dim, norm=not mixer_a)
        qkv_s = _qkv_proj_rope(hs, *bf16["qkv", i], colscale, q_dim, kv_dim, norm=not mixer_a,
                               rope_tables=rope_tables)
        ks_out.append(kv32[:, :kv_dim].reshape(batch_p, seq_p, n_kv, head_dim))
        vs_out.append(kv32[:, kv_dim:].reshape(batch_p, seq_p, n_kv, head_dim))
        op = _self_attention(qkv_p, qkv_p[:, q_dim + kv_dim:].T, sink, batch_p, q_dim, kv_dim,
                             has_sink=mixer_a)
        v_s = qkv_s[:, q_dim + kv_dim:]
        if mixer_a:
            v_t = v_s.reshape(m_s // LANE, LANE, n_kv, LANE).transpose(2, 0, 3, 1)
            osm = _window_attention(qkv_s, v_t, k_ctx, v_ctx_t, i, sink, batch_s, q_dim, kv_dim)
        else:
            osm = _dense_attention(qkv_s, v_s.T, k_ctx, v_ctx_t, i, batch_s, q_dim, kv_dim)
        xp = _proj_residual(op, *bf16["o", i], xp, mod_p, 2, TILES["proj_m"], TILES["proj_n"])
        xs = _proj_residual(osm, *bf16["o", i], xs, mod_s, 2, TILES["proj_m"], TILES["proj_n"])

        hp = _norm_mod(xp, gains_ffn, i, mod_p, 3)
        hs = _norm_mod(xs, gains_ffn, i, mod_s, 3)
        cast_names = ["down"] + (["qkv", "o", "gate", "up"] if i + 1 < depth else [])
        zs, cast = _gate_up(hs, bf16["gate", i], bf16["up", i], cw, cb, i, seq_s, ffn_m_s,
                            casts=[(f32_weights[name], i if name == "down" else i + 1) for name in cast_names])
        for name, w in zip(cast_names, cast):
            bf16[name, i if name == "down" else i + 1] = (w[None], 0)
        zp, _ = _gate_up(hp, bf16["gate", i], bf16["up", i], cw, cb, i, seq_p, ffn_m_p)
        xp = _proj_residual(zp, *bf16["down", i], xp, mod_p, 5, TILES["down_m"], TILES["down_n"])
        xs = _proj_residual(zs, *bf16["down", i], xs, mod_s, 5, TILES["down_m"], TILES["down_n"])

    y_prompt = _final_norm(xp, norm_f).reshape(batch_p, seq_p, d)
    y_sample = _final_norm(xs, norm_f).reshape(batch_s, seq_s, d)
    return y_prompt, y_sample, jnp.stack(ks_out, axis=1), jnp.stack(vs_out, axis=1)
```

```python
import functools
import math

import jax
import jax.numpy as jnp
from jax import lax
from jax.experimental import pallas as pl
from jax.experimental.pallas import tpu as pltpu

GRID_W = 64
WINDOW = 128
ROPE_THETA = 10000.0
EPS = 1e-6
LANE = 128
SUB = 8
PACK = 16
NORM_ROWS = PACK
NORM_UNROLL = 8
NEG_BIG = -1e30
UNDERFLOW_SUM = 2.0 ** -60
LOG2E = math.log2(math.e)
VMEM_LIMIT_BYTES = 58 * 1024 * 1024

BF16 = jnp.bfloat16
F32 = jnp.float32

TILES = dict(
    mod_n=512,
    norm_m=512,
    qkv_m=1024, qkv_n=1024, qkv_rope_m=512,
    proj_m=1024, proj_n=1024,
    ffn_m=1024, ffn_f=512,
    down_m=512, down_n=512,
    win_q=256,
    attn_q=256, attn_kv=2048,
)


def _tile(dim, pref, mult):
    if dim <= pref:
        return dim
    t = pref - pref % mult
    while t >= mult:
        if dim % t == 0:
            return t
        t -= mult
    return dim


def _params(*semantics):
    return pltpu.CompilerParams(dimension_semantics=semantics, vmem_limit_bytes=VMEM_LIMIT_BYTES)


def _silu(x):
    return x / (1.0 + jnp.exp(-x))


def _dot(a, b):
    return jnp.dot(a, b, preferred_element_type=F32)


def _dot_nt(a, b):
    return lax.dot_general(a, b, (((1,), (1,)), ((), ())), preferred_element_type=F32)


def _mod_kernel(c_ref, w_ref, b_ref, o_ref):
    a = _silu(c_ref[...]).astype(BF16)
    o_ref[...] = _dot(a, w_ref[...].astype(BF16)) + b_ref[...]


def _modulation(cond, w_mod, b_mod):
    depth, d, n = w_mod.shape
    rows = cond.shape[0]
    tn = _tile(n, TILES["mod_n"], LANE)
    return pl.pallas_call(
        _mod_kernel,
        grid=(depth, n // tn),
        in_specs=[
            pl.BlockSpec((rows, d), lambda l, j: (0, 0)),
            pl.BlockSpec((None, d, tn), lambda l, j: (l, 0, j)),
            pl.BlockSpec((None, 1, tn), lambda l, j: (l, 0, j)),
        ],
        out_specs=pl.BlockSpec((None, rows, tn), lambda l, j: (l, 0, j)),
        out_shape=jax.ShapeDtypeStruct((depth, rows, n), F32),
        compiler_params=_params("parallel", "parallel"),
        name="modulation",
    )(cond, w_mod, b_mod.reshape(depth, 1, n))


class _Mod:
    def __init__(self, table, base, rows_per_group):
        self.table, self.base, self.rows_per_group = table, base, rows_per_group

    def spec(self, slot, bm, bn, nj):
        base, rpg = self.base, self.rows_per_group
        if nj == 1:
            return pl.BlockSpec((None, 1, bn), lambda i, *_: (base + (i * bm) // rpg, 0, slot))
        return pl.BlockSpec((None, 1, bn), lambda i, j: (base + (i * bm) // rpg, 0, slot * nj + j))


def _rms_rows(x_ref, o_ref, scale_ref, shift_ref):
    def body(r, carry):
        rows = pl.ds(pl.multiple_of(r * NORM_ROWS, NORM_ROWS), NORM_ROWS)
        x = x_ref[rows, :]
        inv = lax.rsqrt(jnp.mean(x * x, axis=-1, keepdims=True) + EPS)
        y = x_ref[rows, :] * inv * scale_ref[...]
        if shift_ref is not None:
            y = y + shift_ref[...]
        o_ref[rows, :] = y.astype(o_ref.dtype)
        return carry

    lax.fori_loop(0, x_ref.shape[0] // NORM_ROWS, body, 0, unroll=NORM_UNROLL)


def _norm_mod_kernel(x_ref, g_ref, sh_ref, sc_ref, o_ref, gs_ref):
    gs_ref[...] = g_ref[...] * (1.0 + sc_ref[...])
    _rms_rows(x_ref, o_ref, gs_ref, sh_ref)


def _norm_mod(x, gains, layer, mod, shift_slot):
    m, d = x.shape
    bm = _tile(min(m, mod.rows_per_group), TILES["norm_m"], NORM_ROWS)
    return pl.pallas_call(
        _norm_mod_kernel,
        grid=(m // bm,),
        in_specs=[
            pl.BlockSpec((bm, d), lambda i: (i, 0)),
            pl.BlockSpec((None, 1, d), lambda i: (layer, 0, 0)),
            mod.spec(shift_slot, bm, d, 1),
            mod.spec(shift_slot + 1, bm, d, 1),
        ],
        out_specs=pl.BlockSpec((bm, d), lambda i: (i, 0)),
        out_shape=jax.ShapeDtypeStruct((m, d), BF16),
        scratch_shapes=[pltpu.VMEM((1, d), F32)],
        compiler_params=_params("parallel"),
        name="norm_mod",
    )(x, gains, mod.table, mod.table)


def _norm_kernel(x_ref, g_ref, o_ref):
    _rms_rows(x_ref, o_ref, g_ref, None)


def _final_norm(x, gain):
    m, d = x.shape
    bm = _tile(m, TILES["norm_m"], NORM_ROWS)
    return pl.pallas_call(
        _norm_kernel,
        grid=(m // bm,),
        in_specs=[pl.BlockSpec((bm, d), lambda i: (i, 0)), pl.BlockSpec((1, d), lambda i: (0, 0))],
        out_specs=pl.BlockSpec((bm, d), lambda i: (i, 0)),
        out_shape=jax.ShapeDtypeStruct((m, d), F32),
        compiler_params=_params("parallel"),
        name="final_norm",
    )(x, gain.reshape(1, d))


def _qkv_kernel(h_ref, w_ref, cs_ref, o_ref, kv_ref, *, n_qk, norm):
    is_v = pl.program_id(1) >= n_qk
    acc = _dot(h_ref[...], w_ref[...])
    outs = []
    for hh in range(acc.shape[1] // LANE):
        blk = acc[:, hh * LANE:(hh + 1) * LANE]
        if norm:
            inv = lax.rsqrt(jnp.mean(blk * blk, axis=-1, keepdims=True) + EPS)
            blk = blk * jnp.where(is_v, 1.0, inv)
        outs.append(blk * cs_ref[:, hh * LANE:(hh + 1) * LANE])
    res = jnp.concatenate(outs, axis=1) if len(outs) > 1 else outs[0]
    o_ref[...] = res.astype(o_ref.dtype)
    kv_ref[...] = res


def _qkv_proj(h, w, layer, colscale, q_dim, kv_dim, *, norm):
    m, d = h.shape
    n = w.shape[2]
    bm = _tile(m, TILES["qkv_m"], PACK)
    bn = _tile(kv_dim, TILES["qkv_n"], LANE)
    assert q_dim % bn == 0 and kv_dim % bn == 0
    n_q = q_dim // bn
    return pl.pallas_call(
        functools.partial(_qkv_kernel, n_qk=(q_dim + kv_dim) // bn, norm=norm),
        grid=(m // bm, n // bn),
        in_specs=[
            pl.BlockSpec((bm, d), lambda i, j: (i, 0)),
            pl.BlockSpec((None, d, bn), lambda i, j: (layer, 0, j)),
            pl.BlockSpec((1, bn), lambda i, j: (0, j)),
        ],
        out_specs=[pl.BlockSpec((bm, bn), lambda i, j: (i, j)),
                   pl.BlockSpec((bm, bn), lambda i, j: (i, jnp.maximum(j - n_q, 0)))],
        out_shape=[jax.ShapeDtypeStruct((m, n), BF16), jax.ShapeDtypeStruct((m, 2 * kv_dim), F32)],
        compiler_params=_params("parallel", "arbitrary"),
        name="qkv_proj",
    )(h, w, colscale)


def _head_epilogue(acc, cs_ref, cos, sin, is_v, norm):
    lane = lax.broadcasted_iota(jnp.int32, (1, LANE), 1)
    first_half = (lane % (LANE // 2)) < (LANE // 4)
    outs = []
    for hh in range(acc.shape[1] // LANE):
        blk = acc[:, hh * LANE:(hh + 1) * LANE]
        if norm:
            inv = lax.rsqrt(jnp.mean(blk * blk, axis=-1, keepdims=True) + EPS)
            blk = blk * jnp.where(is_v, 1.0, inv)
        blk = blk * cs_ref[:, hh * LANE:(hh + 1) * LANE]
        partner = jnp.where(first_half, pltpu.roll(blk, LANE - LANE // 4, 1),
                            pltpu.roll(blk, LANE // 4, 1))
        outs.append(blk * cos + partner * sin)
    return jnp.concatenate(outs, axis=1) if len(outs) > 1 else outs[0]


def _qkv_lag_kernel(h_ref, w_ref, cs_ref, cos_ref, sin_ref, o_ref, acc_ref, *, n_i, n_qk, norm):
    s = pl.program_id(0)

    @pl.when(s == 0)
    def _():
        acc_ref[...] = jnp.zeros_like(acc_ref)

    is_v = jnp.maximum(s - 1, 0) // n_i >= n_qk
    cos = jnp.where(is_v, 1.0, cos_ref[...])
    sin = jnp.where(is_v, 0.0, sin_ref[...])
    o_ref[...] = _head_epilogue(acc_ref[...], cs_ref, cos, sin, is_v, norm).astype(o_ref.dtype)
    acc_ref[...] = _dot(h_ref[...], w_ref[...])


def _qkv_proj_rope(h, w, layer, colscale, q_dim, kv_dim, *, norm, rope_tables):
    m, d = h.shape
    n = w.shape[2]
    bm = _tile(m, TILES["qkv_rope_m"], PACK)
    bn = _tile(kv_dim, TILES["qkv_n"], LANE)
    assert q_dim % bn == 0 and kv_dim % bn == 0
    n_i = m // bm
    steps = n_i * (n // bn)
    cos, sin = rope_tables
    per = cos.shape[0] // bm
    assert cos.shape[0] % bm == 0
    cur = lambda s: jnp.minimum(s, steps - 1)
    fin = lambda s: jnp.maximum(s - 1, 0)
    return pl.pallas_call(
        functools.partial(_qkv_lag_kernel, n_i=n_i, n_qk=(q_dim + kv_dim) // bn, norm=norm),
        grid=(steps + 1,),
        in_specs=[
            pl.BlockSpec((bm, d), lambda s: (cur(s) % n_i, 0)),
            pl.BlockSpec((None, d, bn), lambda s: (layer, 0, cur(s) // n_i)),
            pl.BlockSpec((1, bn), lambda s: (0, fin(s) // n_i)),
            pl.BlockSpec((bm, LANE), lambda s: ((fin(s) % n_i) % per, 0)),
            pl.BlockSpec((bm, LANE), lambda s: ((fin(s) % n_i) % per, 0)),
        ],
        out_specs=pl.BlockSpec((bm, bn), lambda s: (fin(s) % n_i, fin(s) // n_i)),
        out_shape=jax.ShapeDtypeStruct((m, n), BF16),
        scratch_shapes=[pltpu.VMEM((bm, bn), F32)],
        compiler_params=_params("arbitrary"),
        name="qkv_proj_rope",
    )(h, w, colscale, cos, sin)


def _stack_heads(q_ref, r_heads):
    return jnp.concatenate([q_ref[:, r * LANE:(r + 1) * LANE] for r in range(r_heads)], axis=0)


def _store_heads_t(o_ref, o_t, r_heads):
    tq = o_t.shape[1] // r_heads
    for r in range(r_heads):
        o_ref[:, r * LANE:(r + 1) * LANE] = o_t[:, r * tq:(r + 1) * tq].T.astype(o_ref.dtype)


def _max_key_norm(*k_refs):
    def max_sq_norm(ref):
        x = ref[...].astype(F32)
        return jnp.max(jnp.sum(x * x, axis=1, keepdims=True))
    return jnp.sqrt(functools.reduce(jnp.maximum, [max_sq_norm(r) for r in k_refs]))


def _sink_row(sink_ref, g, r_heads, tq):
    return jnp.concatenate(
        [jnp.full((1, tq), sink_ref[g * r_heads + r], F32) for r in range(r_heads)], axis=1)


def _attend(q_ref, o_ref, segments, k_max, sink, r_heads):
    qs = _stack_heads(q_ref, r_heads)
    rows = qs.shape[0]
    q_sq = _dot_nt(jnp.ones((SUB, LANE), BF16), qs * qs)[0:1]
    bound = jnp.sqrt(q_sq) * k_max
    if sink is not None:
        bound = jnp.maximum(bound, sink)

    def scores(k, bias):
        s = _dot_nt(k, qs)
        return s if bias is None else s + bias

    def attempt(off):
        den = jnp.zeros((1, rows), F32)
        acc = jnp.zeros((LANE, rows), F32)
        for k, v_t, bias in segments():
            p = jnp.exp2(scores(k, bias) - off)
            den = den + jnp.sum(p, axis=0, keepdims=True)
            acc = acc + _dot(v_t, p.astype(BF16))
        if sink is not None:
            den = den + jnp.exp2(sink - off)
        return acc * (1.0 / den), den

    out, den = attempt(bound)
    _store_heads_t(o_ref, out, r_heads)

    @pl.when(jnp.min(den) < UNDERFLOW_SUM)
    def _():
        row_max = jnp.full((1, rows), NEG_BIG, F32) if sink is None else sink
        for k, _, bias in segments():
            row_max = jnp.maximum(row_max, jnp.max(scores(k, bias), axis=0, keepdims=True))
        out, _ = attempt(row_max)
        _store_heads_t(o_ref, out, r_heads)


def _self_attn_kernel(sink_ref, q_ref, k_ref, vt_ref, o_ref, kmax_ref, *, r_heads, has_sink):
    kmax_ref[0] = _max_key_norm(k_ref)
    sink = _sink_row(sink_ref, pl.program_id(1), r_heads, q_ref.shape[0]) if has_sink else None
    _attend(q_ref, o_ref, lambda: [(k_ref[...], vt_ref[...], None)], kmax_ref[0], sink, r_heads)


def _self_attention(qkv, v_t, sink, batch, q_dim, kv_dim, *, has_sink):
    m = qkv.shape[0]
    t = m // batch
    n_kv = kv_dim // LANE
    r_heads = q_dim // kv_dim
    qw = r_heads * LANE
    return pl.pallas_call(
        functools.partial(_self_attn_kernel, r_heads=r_heads, has_sink=has_sink),
        grid=(batch, n_kv),
        in_specs=[
            pl.BlockSpec(memory_space=pltpu.SMEM),
            pl.BlockSpec((t, qw), lambda b, g: (b, g)),
            pl.BlockSpec((t, LANE), lambda b, g: (b, q_dim // LANE + g)),
            pl.BlockSpec((LANE, t), lambda b, g: (g, b)),
        ],
        out_specs=pl.BlockSpec((t, qw), lambda b, g: (b, g)),
        out_shape=jax.ShapeDtypeStruct((m, q_dim), BF16),
        scratch_shapes=[pltpu.SMEM((1,), F32)],
        compiler_params=_params("parallel", "parallel"),
        name="self_attention",
    )(sink, qkv, qkv, v_t)


def _win_attn_kernel(sink_ref, q_ref, k_ref, vt_ref, kc_ref, vct_ref, bias_ref, o_ref,
                     kmax_ref, *, r_heads, span):
    i = pl.program_id(2)
    tq = q_ref.shape[0]
    t = k_ref.shape[0]

    @pl.when(i == 0)
    def _():
        kmax_ref[0] = _max_key_norm(k_ref, kc_ref)

    base = pl.multiple_of(jnp.clip(i * tq - WINDOW, 0, t - span), LANE)

    def segments():
        k_w = k_ref[pl.ds(base, span), :]
        v_w_t = jnp.concatenate([vt_ref[base // LANE + c] for c in range(span // LANE)], axis=1)
        bias = jnp.concatenate([bias_ref[...]] * r_heads, axis=1)
        return [(kc_ref[...], vct_ref[...], None), (k_w, v_w_t, bias)]

    _attend(q_ref, o_ref, segments, kmax_ref[0],
            _sink_row(sink_ref, pl.program_id(1), r_heads, tq), r_heads)


def _window_attention(qkv, v_t, k_ctx, v_ctx_t, layer, sink, batch, q_dim, kv_dim):
    m = qkv.shape[0]
    t = m // batch
    n_kv = kv_dim // LANE
    r_heads = q_dim // kv_dim
    qw = r_heads * LANE
    tq = _tile(t, TILES["win_q"], LANE)
    reach = -(-WINDOW // LANE) * LANE
    span = tq + 2 * reach
    assert t % tq == 0 and t >= span
    nb = t // tq
    p = k_ctx.shape[3]
    blk = jnp.array([0, min(1, nb - 1), nb - 1], dtype=jnp.int32)[:, None, None]
    base = jnp.clip(blk * tq - reach, 0, t - span)
    kpos = base + jnp.arange(span, dtype=jnp.int32)[None, :, None]
    qpos = blk * tq + jnp.arange(tq, dtype=jnp.int32)[None, None, :]
    bias = jnp.where(jnp.abs(qpos - kpos) <= WINDOW, 0.0, NEG_BIG).astype(F32)
    variant = lambda i: jnp.where(i == 0, 0, jnp.where(i == nb - 1, 2, 1))
    return pl.pallas_call(
        functools.partial(_win_attn_kernel, r_heads=r_heads, span=span),
        grid=(batch, n_kv, nb),
        in_specs=[
            pl.BlockSpec(memory_space=pltpu.SMEM),
            pl.BlockSpec((tq, qw), lambda b, g, i: (b * nb + i, g)),
            pl.BlockSpec((t, LANE), lambda b, g, i: (b, q_dim // LANE + g)),
            pl.BlockSpec((None, t // LANE, LANE, LANE), lambda b, g, i: (g, b, 0, 0)),
            pl.BlockSpec((None, None, None, p, LANE), lambda b, g, i: (b, layer, g, 0, 0)),
            pl.BlockSpec((None, None, None, LANE, p), lambda b, g, i: (b, layer, g, 0, 0)),
            pl.BlockSpec((None, span, tq), lambda b, g, i: (variant(i), 0, 0)),
        ],
        out_specs=pl.BlockSpec((tq, qw), lambda b, g, i: (b * nb + i, g)),
        out_shape=jax.ShapeDtypeStruct((m, q_dim), BF16),
        scratch_shapes=[pltpu.SMEM((1,), F32)],
        compiler_params=_params("parallel", "parallel", "arbitrary"),
        name="window_attention",
    )(sink, qkv, qkv, v_t, k_ctx, v_ctx_t, bias)


def _dense_attn_kernel(q_ref, k_ref, vt_ref, kc_ref, vct_ref, o_ref, kmax_ref, *, r_heads, kv_chunk):
    t = k_ref.shape[0]

    @pl.when(pl.program_id(2) == 0)
    def _():
        kmax_ref[0] = _max_key_norm(k_ref, kc_ref)

    def segments():
        return [(kc_ref[...], vct_ref[...], None)] + [
            (k_ref[c * kv_chunk:(c + 1) * kv_chunk, :], vt_ref[:, c * kv_chunk:(c + 1) * kv_chunk], None)
            for c in range(t // kv_chunk)]

    _attend(q_ref, o_ref, segments, kmax_ref[0], None, r_heads)


def _dense_attention(qkv, v_t, k_ctx, v_ctx_t, layer, batch, q_dim, kv_dim):
    m = qkv.shape[0]
    t = m // batch
    n_kv = kv_dim // LANE
    r_heads = q_dim // kv_dim
    qw = r_heads * LANE
    tq = _tile(t, TILES["attn_q"], LANE)
    kv_chunk = _tile(t, TILES["attn_kv"], LANE)
    nb = t // tq
    p = k_ctx.shape[3]
    return pl.pallas_call(
        functools.partial(_dense_attn_kernel, r_heads=r_heads, kv_chunk=kv_chunk),
        grid=(batch, n_kv, nb),
        in_specs=[
            pl.BlockSpec((tq, qw), lambda b, g, i: (b * nb + i, g)),
            pl.BlockSpec((t, LANE), lambda b, g, i: (b, q_dim // LANE + g)),
            pl.BlockSpec((LANE, t), lambda b, g, i: (g, b)),
            pl.BlockSpec((None, None, None, p, LANE), lambda b, g, i: (b, layer, g, 0, 0)),
            pl.BlockSpec((None, None, None, LANE, p), lambda b, g, i: (b, layer, g, 0, 0)),
        ],
        out_specs=pl.BlockSpec((tq, qw), lambda b, g, i: (b * nb + i, g)),
        out_shape=jax.ShapeDtypeStruct((m, q_dim), BF16),
        scratch_shapes=[pltpu.SMEM((1,), F32)],
        compiler_params=_params("parallel", "parallel", "arbitrary"),
        name="dense_attention",
    )(qkv, qkv, v_t, k_ctx, v_ctx_t)


def _proj_res_kernel(a_ref, w_ref, x_ref, g_ref, o_ref):
    o_ref[...] = x_ref[...] + g_ref[...] * _dot(a_ref[...], w_ref[...])


def _proj_residual(a, w, layer, x, mod, gate_slot, bm_pref, bn_pref):
    m, k = a.shape
    d = w.shape[2]
    bm = _tile(min(m, mod.rows_per_group), bm_pref, PACK)
    bn = _tile(d, bn_pref, LANE)
    nj = d // bn
    gate = mod.spec(gate_slot, bm, bn, nj)
    return pl.pallas_call(
        _proj_res_kernel,
        grid=(nj, m // bm),
        in_specs=[
            pl.BlockSpec((bm, k), lambda j, i: (i, 0)),
            pl.BlockSpec((None, k, bn), lambda j, i: (layer, 0, j)),
            pl.BlockSpec((bm, bn), lambda j, i: (i, j)),
            pl.BlockSpec(gate.block_shape, lambda j, i: gate.index_map(i, j)),
        ],
        out_specs=pl.BlockSpec((bm, bn), lambda j, i: (i, j)),
        out_shape=jax.ShapeDtypeStruct((m, d), F32),
        compiler_params=_params("parallel", "parallel"),
        name="proj_residual",
    )(a, w, x, mod.table)


def _matmul_kernel(a_ref, w_ref, o_ref):
    o_ref[...] = _dot(a_ref[...], w_ref[...])


def _edge_rows_kernel(before_ref, after_ref, o_ref):
    row = lax.broadcasted_iota(jnp.int32, o_ref.shape, 0)
    last = before_ref[...].astype(F32)[PACK - 1:PACK, :]
    first = after_ref[...].astype(F32)[0:1, :]
    o_ref[...] = jnp.where(row == SUB - 1, last, jnp.where(row == SUB, first, 0.0)).astype(o_ref.dtype)


def _edge_gate(h, wg, layer, bm):
    m, d = h.shape
    nt = m // bm
    d_ff = wg.shape[2]
    per = bm // PACK
    edge = pl.pallas_call(
        _edge_rows_kernel,
        grid=(nt,),
        in_specs=[pl.BlockSpec((PACK, d), lambda i: (jnp.maximum(i * per - 1, 0), 0)),
                  pl.BlockSpec((PACK, d), lambda i: (jnp.minimum((i + 1) * per, nt * per - 1), 0))],
        out_specs=pl.BlockSpec((None, PACK, d), lambda i: (i, 0, 0)),
        out_shape=jax.ShapeDtypeStruct((nt, PACK, d), h.dtype),
        compiler_params=_params("parallel"),
        name="edge_rows",
    )(h, h)
    bn = min(d_ff, TILES["ffn_f"])
    out = pl.pallas_call(
        _matmul_kernel,
        grid=(pl.cdiv(d_ff, bn),),
        in_specs=[pl.BlockSpec((nt * PACK, d), lambda j: (0, 0)),
                  pl.BlockSpec((None, d, bn), lambda j: (layer, 0, j))],
        out_specs=pl.BlockSpec((nt * PACK, bn), lambda j: (0, j)),
        out_shape=jax.ShapeDtypeStruct((nt * PACK, d_ff), F32),
        compiler_params=_params("parallel"),
        name="edge_gate",
    )(edge.reshape(nt * PACK, d), wg)
    return out.reshape(nt, PACK, d_ff)


def _gate_up_kernel(*refs, bm, seq_len, n_cast):
    h_ref, wg_ref, wu_ref, cw_ref, cb_ref, e_ref = refs[:6]
    cast_in = refs[6:6 + n_cast]
    o_ref = refs[6 + n_cast]
    cast_out = refs[7 + n_cast:7 + 2 * n_cast]
    a_ref = refs[7 + 2 * n_cast]
    for src, dst in zip(cast_in, cast_out):
        dst[...] = src[...].astype(dst.dtype)
    a_ref[0:SUB, :] = e_ref[0:SUB, :]
    a_ref[SUB:SUB + bm, :] = _dot(h_ref[...], wg_ref[...])
    a_ref[SUB + bm:, :] = e_ref[SUB:, :]
    u = _dot(h_ref[...], wu_ref[...])
    row = pl.program_id(0) * bm + lax.broadcasted_iota(jnp.int32, (bm, 1), 0)
    pos = row % seq_len
    prev = jnp.where(pos == 0, 0.0, a_ref[SUB - 1:SUB - 1 + bm, :])
    nxt = jnp.where(pos == seq_len - 1, 0.0, a_ref[SUB + 1:SUB + 1 + bm, :])
    cur = a_ref[SUB:SUB + bm, :]
    a = prev * cw_ref[0:1, :] + cur * cw_ref[1:2, :] + nxt * cw_ref[2:3, :] + cb_ref[...]
    o_ref[...] = (_silu(a) * u).astype(o_ref.dtype)


def _gate_up(h, wg, wu, conv_w, conv_b, layer, seq_len, bm, casts=()):
    m, d = h.shape
    (wg_arr, wg_l), (wu_arr, wu_l) = wg, wu
    d_ff = wg_arr.shape[2]
    bf = min(d_ff, TILES["ffn_f"])
    n_f = pl.cdiv(d_ff, bf)
    steps = (m // bm) * n_f
    edge = _edge_gate(h, wg_arr, wg_l, bm)
    cast_in_specs, cast_out_specs, cast_shapes = [], [], []
    for arr, l in casts:
        _, r, c = arr.shape
        rows = PACK * pl.cdiv(r, PACK * steps)
        blk = lambda i, f, n=pl.cdiv(r, rows): jnp.minimum(i * n_f + f, n - 1)
        cast_in_specs.append(pl.BlockSpec((None, rows, c), lambda i, f, l=l, blk=blk: (l, blk(i, f), 0)))
        cast_out_specs.append(pl.BlockSpec((rows, c), lambda i, f, blk=blk: (blk(i, f), 0)))
        cast_shapes.append(jax.ShapeDtypeStruct((r, c), BF16))
    out = pl.pallas_call(
        functools.partial(_gate_up_kernel, bm=bm, seq_len=seq_len, n_cast=len(casts)),
        grid=(m // bm, n_f),
        in_specs=[
            pl.BlockSpec((bm, d), lambda i, f: (i, 0)),
            pl.BlockSpec((None, d, bf), lambda i, f: (wg_l, 0, f)),
            pl.BlockSpec((None, d, bf), lambda i, f: (wu_l, 0, f)),
            pl.BlockSpec((None, conv_w.shape[1], bf), lambda i, f: (layer, 0, f)),
            pl.BlockSpec((None, 1, bf), lambda i, f: (layer, 0, f)),
            pl.BlockSpec((None, PACK, bf), lambda i, f: (i, 0, f)),
        ] + cast_in_specs,
        out_specs=[pl.BlockSpec((bm, bf), lambda i, f: (i, f))] + cast_out_specs,
        out_shape=[jax.ShapeDtypeStruct((m, d_ff), BF16)] + cast_shapes,
        scratch_shapes=[pltpu.VMEM((bm + PACK, bf), F32)],
        compiler_params=_params("arbitrary", "arbitrary"),
        name="gate_up",
    )(h, wg_arr, wu_arr, conv_w, conv_b, edge, *[arr for arr, _ in casts])
    return out[0], list(out[1:])


def _rope_tables(n_tokens):
    axis_dim = LANE // 2
    half = axis_dim // 2
    rows = n_tokens // GRID_W
    row = jnp.repeat(jnp.arange(rows, dtype=jnp.int32), GRID_W)
    col = jnp.tile(jnp.arange(GRID_W, dtype=jnp.int32), rows)
    inv_freq = ROPE_THETA ** (-jnp.arange(half, dtype=F32) / half)
    cos_parts, sin_parts = [], []
    for pos in (row, col):
        ang = pos.astype(F32)[:, None] * inv_freq[None, :]
        c, s = jnp.cos(ang), jnp.sin(ang)
        cos_parts += [c, c]
        sin_parts += [-s, s]
    return jnp.concatenate(cos_parts, axis=1), jnp.concatenate(sin_parts, axis=1)


def kernel(x_prompt, x_sample, cache_k, cache_v, c, c_ctx, w_mod, b_mod, norm_attn, norm_ffn, w_qkv, w_o,
           sink_a, q_norm_b, k_norm_b, w_gate, w_up, w_down, conv_w, conv_b, norm_f):
    batch_p, seq_p, d = x_prompt.shape
    batch_s, seq_s, _ = x_sample.shape
    depth = w_mod.shape[0]
    n_kv, head_dim = cache_k.shape[3], cache_k.shape[4]
    n_heads = sink_a.shape[1]
    assert head_dim == LANE
    q_dim, kv_dim = n_heads * head_dim, n_kv * head_dim
    d_ff = w_gate.shape[2]
    scale = head_dim ** -0.5
    m_p, m_s = batch_p * seq_p, batch_s * seq_s

    ffn_m_p = _tile(m_p, TILES["ffn_m"], seq_p)
    ffn_m_s = _tile(seq_s, TILES["ffn_m"], PACK)

    xp = x_prompt.reshape(m_p, d)
    xs = x_sample.reshape(m_s, d)

    n_cond = -(-(batch_s + 1) // SUB) * SUB
    cond = jnp.zeros((n_cond, d), F32).at[:batch_s].set(c).at[batch_s].set(c_ctx)
    mod_table = _modulation(cond, w_mod, b_mod).reshape(depth * n_cond, 1, -1)

    f32_weights = dict(qkv=w_qkv, o=w_o, gate=w_gate, up=w_up, down=w_down)
    bf16 = {(name, 0): (f32_weights[name][:1].astype(BF16), 0) for name in ("qkv", "o", "gate", "up")}
    cw = conv_w
    cb = conv_b.reshape(depth, 1, d_ff)
    gains_attn = norm_attn.reshape(depth, 1, d)
    gains_ffn = norm_ffn.reshape(depth, 1, d)
    k_ctx = cache_k.transpose(0, 1, 3, 2, 4).astype(BF16)
    v_ctx_t = cache_v.transpose(0, 1, 3, 4, 2).astype(BF16)

    rope_tables = _rope_tables(seq_s)
    ones_kv = jnp.ones((kv_dim,), F32)
    ks_out, vs_out = [], []
    for i in range(depth):
        mod_s = _Mod(mod_table, i * n_cond, seq_s)
        mod_p = _Mod(mod_table, i * n_cond + batch_s, m_p)
        mixer_a = i % 2 == 0
        if mixer_a:
            sink = sink_a[i // 2].astype(F32) * LOG2E
            colscale = jnp.concatenate([jnp.full((q_dim,), scale * LOG2E, F32), ones_kv, ones_kv])
        else:
            sink = jnp.zeros((n_heads,), F32)
            colscale = jnp.concatenate([jnp.tile(q_norm_b[i // 2].astype(F32), n_heads) * (scale * LOG2E),
                                        jnp.tile(k_norm_b[i // 2].astype(F32), n_kv), ones_kv])
        colscale = colscale.reshape(1, -1)

        hp = _norm_mod(xp, gains_attn, i, mod_p, 0)
        hs = _norm_mod(xs, gains_attn, i, mod_s, 0)
        qkv_p, kv32 = _qkv_proj(hp, *bf16["qkv", i], colscale, q_dim, kv_dim, norm=not mixer_a)
        qkv_s = _qkv_proj_rope(hs, *bf16["qkv", i], colscale, q_dim, kv_dim, norm=not mixer_a,
                               rope_tables=rope_tables)
        ks_out.append(kv32[:, :kv_dim].reshape(batch_p, seq_p, n_kv, head_dim))
        vs_out.append(kv32[:, kv_dim:].reshape(batch_p, seq_p, n_kv, head_dim))
        op = _self_attention(qkv_p, qkv_p[:, q_dim + kv_dim:].T, sink, batch_p, q_dim, kv_dim,
                             has_sink=mixer_a)
        v_s = qkv_s[:, q_dim + kv_dim:]
        if mixer_a:
            v_t = v_s.reshape(m_s // LANE, LANE, n_kv, LANE).transpose(2, 0, 3, 1)
            osm = _window_attention(qkv_s, v_t, k_ctx, v_ctx_t, i, sink, batch_s, q_dim, kv_dim)
        else:
            osm = _dense_attention(qkv_s, v_s.T, k_ctx, v_ctx_t, i, batch_s, q_dim, kv_dim)
        xp = _proj_residual(op, *bf16["o", i], xp, mod_p, 2, TILES["proj_m"], TILES["proj_n"])
        xs = _proj_residual(osm, *bf16["o", i], xs, mod_s, 2, TILES["proj_m"], TILES["proj_n"])

        hp = _norm_mod(xp, gains_ffn, i, mod_p, 3)
        hs = _norm_mod(xs, gains_ffn, i, mod_s, 3)
        cast_names = ["down"] + (["qkv", "o", "gate", "up"] if i + 1 < depth else [])
        zs, cast = _gate_up(hs, bf16["gate", i], bf16["up", i], cw, cb, i, seq_s, ffn_m_s,
                            casts=[(f32_weights[name], i if name == "down" else i + 1) for name in cast_names])
        for name, w in zip(cast_names, cast):
            bf16[name, i if name == "down" else i + 1] = (w[None], 0)
        zp, _ = _gate_up(hp, bf16["gate", i], bf16["up", i], cw, cb, i, seq_p, ffn_m_p)
        xp = _proj_residual(zp, *bf16["down", i], xp, mod_p, 5, TILES["down_m"], TILES["down_n"])
        xs = _proj_residual(zs, *bf16["down", i], xs, mod_s, 5, TILES["down_m"], TILES["down_n"])

    y_prompt = _final_norm(xp, norm_f).reshape(batch_p, seq_p, d)
    y_sample = _final_norm(xs, norm_f).reshape(batch_s, seq_s, d)
    return y_prompt, y_sample, jnp.stack(ks_out, axis=1), jnp.stack(vs_out, axis=1)
```

```python
import functools
import math

import jax
import jax.numpy as jnp
from jax import lax
from jax.experimental import pallas as pl
from jax.experimental.pallas import tpu as pltpu

GRID_W = 64
WINDOW = 128
ROPE_THETA = 10000.0
EPS = 1e-6
LANE = 128
SUB = 8
PACK = 16
NORM_ROWS = PACK
NORM_UNROLL = 8
NEG_BIG = -1e30
UNDERFLOW_SUM = 2.0 ** -60
LOG2E = math.log2(math.e)
VMEM_LIMIT_BYTES = 58 * 1024 * 1024

BF16 = jnp.bfloat16
F32 = jnp.float32

TILES = dict(
    mod_n=512,
    norm_m=512,
    qkv_m=1024, qkv_n=1024, qkv_rope_m=512,
    proj_m=1024, proj_n=1024,
    ffn_m=1024, ffn_f=512,
    down_m=512, down_n=512,
    win_q=256,
    attn_q=256, attn_kv=2048,
)


def _tile(dim, pref, mult):
    if dim <= pref:
        return dim
    t = pref - pref % mult
    while t >= mult:
        if dim % t == 0:
            return t
        t -= mult
    return dim


def _params(*semantics):
    return pltpu.CompilerParams(dimension_semantics=semantics, vmem_limit_bytes=VMEM_LIMIT_BYTES)


def _silu(x):
    return x / (1.0 + jnp.exp(-x))


def _dot(a, b):
    return jnp.dot(a, b, preferred_element_type=F32)


def _dot_nt(a, b):
    return lax.dot_general(a, b, (((1,), (1,)), ((), ())), preferred_element_type=F32)


def _mod_kernel(c_ref, w_ref, b_ref, o_ref):
    a = _silu(c_ref[...]).astype(BF16)
    o_ref[...] = _dot(a, w_ref[...].astype(BF16)) + b_ref[...]


def _modulation(cond, w_mod, b_mod):
    depth, d, n = w_mod.shape
    rows = cond.shape[0]
    tn = _tile(n, TILES["mod_n"], LANE)
    return pl.pallas_call(
        _mod_kernel,
        grid=(depth, n // tn),
        in_specs=[
            pl.BlockSpec((rows, d), lambda l, j: (0, 0)),
            pl.BlockSpec((None, d, tn), lambda l, j: (l, 0, j)),
            pl.BlockSpec((None, 1, tn), lambda l, j: (l, 0, j)),
        ],
        out_specs=pl.BlockSpec((None, rows, tn), lambda l, j: (l, 0, j)),
        out_shape=jax.ShapeDtypeStruct((depth, rows, n), F32),
        compiler_params=_params("parallel", "parallel"),
        name="modulation",
    )(cond, w_mod, b_mod.reshape(depth, 1, n))


class _Mod:
    def __init__(self, table, base, rows_per_group):
        self.table, self.base, self.rows_per_group = table, base, rows_per_group

    def spec(self, slot, bm, bn, nj):
        base, rpg = self.base, self.rows_per_group
        if nj == 1:
            return pl.BlockSpec((None, 1, bn), lambda i, *_: (base + (i * bm) // rpg, 0, slot))
        return pl.BlockSpec((None, 1, bn), lambda i, j: (base + (i * bm) // rpg, 0, slot * nj + j))


def _rms_rows(x_ref, o_ref, scale_ref, shift_ref):
    def body(r, carry):
        rows = pl.ds(pl.multiple_of(r * NORM_ROWS, NORM_ROWS), NORM_ROWS)
        x = x_ref[rows, :]
        inv = lax.rsqrt(jnp.mean(x * x, axis=-1, keepdims=True) + EPS)
        y = x_ref[rows, :] * inv * scale_ref[...]
        if shift_ref is not None:
            y = y + shift_ref[...]
        o_ref[rows, :] = y.astype(o_ref.dtype)
        return carry

    lax.fori_loop(0, x_ref.shape[0] // NORM_ROWS, body, 0, unroll=NORM_UNROLL)


def _norm_mod_kernel(x_ref, g_ref, sh_ref, sc_ref, o_ref, gs_ref):
    gs_ref[...] = g_ref[...] * (1.0 + sc_ref[...])
    _rms_rows(x_ref, o_ref, gs_ref, sh_ref)


def _norm_mod(x, gains, layer, mod, shift_slot):
    m, d = x.shape
    bm = _tile(min(m, mod.rows_per_group), TILES["norm_m"], NORM_ROWS)
    return pl.pallas_call(
        _norm_mod_kernel,
        grid=(m // bm,),
        in_specs=[
            pl.BlockSpec((bm, d), lambda i: (i, 0)),
            pl.BlockSpec((None, 1, d), lambda i: (layer, 0, 0)),
            mod.spec(shift_slot, bm, d, 1),
            mod.spec(shift_slot + 1, bm, d, 1),
        ],
        out_specs=pl.BlockSpec((bm, d), lambda i: (i, 0)),
        out_shape=jax.ShapeDtypeStruct((m, d), BF16),
        scratch_shapes=[pltpu.VMEM((1, d), F32)],
        compiler_params=_params("parallel"),
        name="norm_mod",
    )(x, gains, mod.table, mod.table)


def _norm_kernel(x_ref, g_ref, o_ref):
    _rms_rows(x_ref, o_ref, g_ref, None)


def _final_norm(x, gain):
    m, d = x.shape
    bm = _tile(m, TILES["norm_m"], NORM_ROWS)
    return pl.pallas_call(
        _norm_kernel,
        grid=(m // bm,),
        in_specs=[pl.BlockSpec((bm, d), lambda i: (i, 0)), pl.BlockSpec((1, d), lambda i: (0, 0))],
        out_specs=pl.BlockSpec((bm, d), lambda i: (i, 0)),
        out_shape=jax.ShapeDtypeStruct((m, d), F32),
        compiler_params=_params("parallel"),
        name="final_norm",
    )(x, gain.reshape(1, d))


def _qkv_kernel(h_ref, w_ref, cs_ref, o_ref, kv_ref, *, n_qk, norm):
    is_v = pl.program_id(1) >= n_qk
    acc = _dot(h_ref[...], w_ref[...])
    outs = []
    for hh in range(acc.shape[1] // LANE):
        blk = acc[:, hh * LANE:(hh + 1) * LANE]
        if norm:
            inv = lax.rsqrt(jnp.mean(blk * blk, axis=-1, keepdims=True) + EPS)
            blk = blk * jnp.where(is_v, 1.0, inv)
        outs.append(blk * cs_ref[:, hh * LANE:(hh + 1) * LANE])
    res = jnp.concatenate(outs, axis=1) if len(outs) > 1 else outs[0]
    o_ref[...] = res.astype(o_ref.dtype)
    kv_ref[...] = res


def _qkv_proj(h, w, layer, colscale, q_dim, kv_dim, *, norm):
    m, d = h.shape
    n = w.shape[2]
    bm = _tile(m, TILES["qkv_m"], PACK)
    bn = _tile(kv_dim, TILES["qkv_n"], LANE)
    assert q_dim % bn == 0 and kv_dim % bn == 0
    n_q = q_dim // bn
    return pl.pallas_call(
        functools.partial(_qkv_kernel, n_qk=(q_dim + kv_dim) // bn, norm=norm),
        grid=(m // bm, n // bn),
        in_specs=[
            pl.BlockSpec((bm, d), lambda i, j: (i, 0)),
            pl.BlockSpec((None, d, bn), lambda i, j: (layer, 0, j)),
            pl.BlockSpec((1, bn), lambda i, j: (0, j)),
        ],
        out_specs=[pl.BlockSpec((bm, bn), lambda i, j: (i, j)),
                   pl.BlockSpec((bm, bn), lambda i, j: (i, jnp.maximum(j - n_q, 0)))],
        out_shape=[jax.ShapeDtypeStruct((m, n), BF16), jax.ShapeDtypeStruct((m, 2 * kv_dim), F32)],
        compiler_params=_params("parallel", "arbitrary"),
        name="qkv_proj",
    )(h, w, colscale)


def _head_epilogue(acc, cs_ref, cos, sin, is_v, norm):
    lane = lax.broadcasted_iota(jnp.int32, (1, LANE), 1)
    first_half = (lane % (LANE // 2)) < (LANE // 4)
    outs = []
    for hh in range(acc.shape[1] // LANE):
        blk = acc[:, hh * LANE:(hh + 1) * LANE]
        if norm:
            inv = lax.rsqrt(jnp.mean(blk * blk, axis=-1, keepdims=True) + EPS)
            blk = blk * jnp.where(is_v, 1.0, inv)
        blk = blk * cs_ref[:, hh * LANE:(hh + 1) * LANE]
        partner = jnp.where(first_half, pltpu.roll(blk, LANE - LANE // 4, 1),
                            pltpu.roll(blk, LANE // 4, 1))
        outs.append(blk * cos + partner * sin)
    return jnp.concatenate(outs, axis=1) if len(outs) > 1 else outs[0]


def _cast_specs(casts, steps, step_of):
    in_specs, out_specs, shapes = [], [], []
    for arr, l in casts:
        _, r, c = arr.shape
        rows = PACK * pl.cdiv(r, PACK * steps)
        blk = lambda *ids, n=pl.cdiv(r, rows): jnp.minimum(step_of(*ids), n - 1)
        in_specs.append(pl.BlockSpec((None, rows, c), lambda *ids, l=l, blk=blk: (l, blk(*ids), 0)))
        out_specs.append(pl.BlockSpec((rows, c), lambda *ids, blk=blk: (blk(*ids), 0)))
        shapes.append(jax.ShapeDtypeStruct((r, c), BF16))
    return in_specs, out_specs, shapes


def _cast_rows(cast_in, cast_out):
    for src, dst in zip(cast_in, cast_out):
        dst[...] = src[...].astype(dst.dtype)


def _qkv_lag_kernel(*refs, n_i, n_qk, norm, n_cast):
    h_ref, w_ref, cs_ref, cos_ref, sin_ref = refs[:5]
    o_ref = refs[5 + n_cast]
    acc_ref = refs[6 + 2 * n_cast]
    _cast_rows(refs[5:5 + n_cast], refs[6 + n_cast:6 + 2 * n_cast])
    s = pl.program_id(0)

    @pl.when(s == 0)
    def _():
        acc_ref[...] = jnp.zeros_like(acc_ref)

    is_v = jnp.maximum(s - 1, 0) // n_i >= n_qk
    cos = jnp.where(is_v, 1.0, cos_ref[...])
    sin = jnp.where(is_v, 0.0, sin_ref[...])
    o_ref[...] = _head_epilogue(acc_ref[...], cs_ref, cos, sin, is_v, norm).astype(o_ref.dtype)
    acc_ref[...] = _dot(h_ref[...], w_ref[...])


def _qkv_proj_rope(h, w, layer, colscale, q_dim, kv_dim, *, norm, rope_tables, casts=()):
    m, d = h.shape
    n = w.shape[2]
    bm = _tile(m, TILES["qkv_rope_m"], PACK)
    bn = _tile(kv_dim, TILES["qkv_n"], LANE)
    assert q_dim % bn == 0 and kv_dim % bn == 0
    n_i = m // bm
    steps = n_i * (n // bn)
    cos, sin = rope_tables
    per = cos.shape[0] // bm
    assert cos.shape[0] % bm == 0
    cur = lambda s: jnp.minimum(s, steps - 1)
    fin = lambda s: jnp.maximum(s - 1, 0)
    cast_in, cast_out, cast_shapes = _cast_specs(casts, steps + 1, lambda s: s)
    out = pl.pallas_call(
        functools.partial(_qkv_lag_kernel, n_i=n_i, n_qk=(q_dim + kv_dim) // bn, norm=norm,
                          n_cast=len(casts)),
        grid=(steps + 1,),
        in_specs=[
            pl.BlockSpec((bm, d), lambda s: (cur(s) % n_i, 0)),
            pl.BlockSpec((None, d, bn), lambda s: (layer, 0, cur(s) // n_i)),
            pl.BlockSpec((1, bn), lambda s: (0, fin(s) // n_i)),
            pl.BlockSpec((bm, LANE), lambda s: ((fin(s) % n_i) % per, 0)),
            pl.BlockSpec((bm, LANE), lambda s: ((fin(s) % n_i) % per, 0)),
        ] + cast_in,
        out_specs=[pl.BlockSpec((bm, bn), lambda s: (fin(s) % n_i, fin(s) // n_i))] + cast_out,
        out_shape=[jax.ShapeDtypeStruct((m, n), BF16)] + cast_shapes,
        scratch_shapes=[pltpu.VMEM((bm, bn), F32)],
        compiler_params=_params("arbitrary"),
        name="qkv_proj_rope",
    )(h, w, colscale, cos, sin, *[arr for arr, _ in casts])
    return out[0], list(out[1:])


def _stack_heads(q_ref, r_heads):
    return jnp.concatenate([q_ref[:, r * LANE:(r + 1) * LANE] for r in range(r_heads)], axis=0)


def _store_heads_t(o_ref, o_t, r_heads):
    tq = o_t.shape[1] // r_heads
    for r in range(r_heads):
        o_ref[:, r * LANE:(r + 1) * LANE] = o_t[:, r * tq:(r + 1) * tq].T.astype(o_ref.dtype)


def _max_key_norm(*k_refs):
    def max_sq_norm(ref):
        x = ref[...].astype(F32)
        return jnp.max(jnp.sum(x * x, axis=1, keepdims=True))
    return jnp.sqrt(functools.reduce(jnp.maximum, [max_sq_norm(r) for r in k_refs]))


def _sink_row(sink_ref, g, r_heads, tq):
    return jnp.concatenate(
        [jnp.full((1, tq), sink_ref[g * r_heads + r], F32) for r in range(r_heads)], axis=1)


def _attend(q_ref, o_ref, segments, k_max, sink, r_heads):
    qs = _stack_heads(q_ref, r_heads)
    rows = qs.shape[0]
    q_sq = _dot_nt(jnp.ones((SUB, LANE), BF16), qs * qs)[0:1]
    bound = jnp.sqrt(q_sq) * k_max
    if sink is not None:
        bound = jnp.maximum(bound, sink)

    def scores(k, bias):
        s = _dot_nt(k, qs)
        return s if bias is None else s + bias

    def attempt(off):
        den = jnp.zeros((1, rows), F32)
        acc = jnp.zeros((LANE, rows), F32)
        for k, v_t, bias in segments():
            p = jnp.exp2(scores(k, bias) - off)
            den = den + jnp.sum(p, axis=0, keepdims=True)
            acc = acc + _dot(v_t, p.astype(BF16))
        if sink is not None:
            den = den + jnp.exp2(sink - off)
        return acc * (1.0 / den), den

    out, den = attempt(bound)
    _store_heads_t(o_ref, out, r_heads)

    @pl.when(jnp.min(den) < UNDERFLOW_SUM)
    def _():
        row_max = jnp.full((1, rows), NEG_BIG, F32) if sink is None else sink
        for k, _, bias in segments():
            row_max = jnp.maximum(row_max, jnp.max(scores(k, bias), axis=0, keepdims=True))
        out, _ = attempt(row_max)
        _store_heads_t(o_ref, out, r_heads)


def _self_attn_kernel(sink_ref, q_ref, k_ref, vt_ref, o_ref, kmax_ref, *, r_heads, has_sink):
    kmax_ref[0] = _max_key_norm(k_ref)
    sink = _sink_row(sink_ref, pl.program_id(1), r_heads, q_ref.shape[0]) if has_sink else None
    _attend(q_ref, o_ref, lambda: [(k_ref[...], vt_ref[...], None)], kmax_ref[0], sink, r_heads)


def _self_attention(qkv, v_t, sink, batch, q_dim, kv_dim, *, has_sink):
    m = qkv.shape[0]
    t = m // batch
    n_kv = kv_dim // LANE
    r_heads = q_dim // kv_dim
    qw = r_heads * LANE
    return pl.pallas_call(
        functools.partial(_self_attn_kernel, r_heads=r_heads, has_sink=has_sink),
        grid=(batch, n_kv),
        in_specs=[
            pl.BlockSpec(memory_space=pltpu.SMEM),
            pl.BlockSpec((t, qw), lambda b, g: (b, g)),
            pl.BlockSpec((t, LANE), lambda b, g: (b, q_dim // LANE + g)),
            pl.BlockSpec((LANE, t), lambda b, g: (g, b)),
        ],
        out_specs=pl.BlockSpec((t, qw), lambda b, g: (b, g)),
        out_shape=jax.ShapeDtypeStruct((m, q_dim), BF16),
        scratch_shapes=[pltpu.SMEM((1,), F32)],
        compiler_params=_params("parallel", "parallel"),
        name="self_attention",
    )(sink, qkv, qkv, v_t)


def _win_attn_kernel(sink_ref, q_ref, k_ref, vt_ref, kc_ref, vct_ref, bias_ref, o_ref,
                     kmax_ref, *, r_heads, span):
    i = pl.program_id(2)
    tq = q_ref.shape[0]
    t = k_ref.shape[0]

    @pl.when(i == 0)
    def _():
        kmax_ref[0] = _max_key_norm(k_ref, kc_ref)

    base = pl.multiple_of(jnp.clip(i * tq - WINDOW, 0, t - span), LANE)

    def segments():
        k_w = k_ref[pl.ds(base, span), :]
        v_w_t = jnp.concatenate([vt_ref[base // LANE + c] for c in range(span // LANE)], axis=1)
        bias = jnp.concatenate([bias_ref[...]] * r_heads, axis=1)
        return [(kc_ref[...], vct_ref[...], None), (k_w, v_w_t, bias)]

    _attend(q_ref, o_ref, segments, kmax_ref[0],
            _sink_row(sink_ref, pl.program_id(1), r_heads, tq), r_heads)


def _window_attention(qkv, v_t, k_ctx, v_ctx_t, layer, sink, batch, q_dim, kv_dim):
    m = qkv.shape[0]
    t = m // batch
    n_kv = kv_dim // LANE
    r_heads = q_dim // kv_dim
    qw = r_heads * LANE
    tq = _tile(t, TILES["win_q"], LANE)
    reach = -(-WINDOW // LANE) * LANE
    span = tq + 2 * reach
    assert t % tq == 0 and t >= span
    nb = t // tq
    p = k_ctx.shape[3]
    blk = jnp.array([0, min(1, nb - 1), nb - 1], dtype=jnp.int32)[:, None, None]
    base = jnp.clip(blk * tq - reach, 0, t - span)
    kpos = base + jnp.arange(span, dtype=jnp.int32)[None, :, None]
    qpos = blk * tq + jnp.arange(tq, dtype=jnp.int32)[None, None, :]
    bias = jnp.where(jnp.abs(qpos - kpos) <= WINDOW, 0.0, NEG_BIG).astype(F32)
    variant = lambda i: jnp.where(i == 0, 0, jnp.where(i == nb - 1, 2, 1))
    return pl.pallas_call(
        functools.partial(_win_attn_kernel, r_heads=r_heads, span=span),
        grid=(batch, n_kv, nb),
        in_specs=[
            pl.BlockSpec(memory_space=pltpu.SMEM),
            pl.BlockSpec((tq, qw), lambda b, g, i: (b * nb + i, g)),
            pl.BlockSpec((t, LANE), lambda b, g, i: (b, q_dim // LANE + g)),
            pl.BlockSpec((None, t // LANE, LANE, LANE), lambda b, g, i: (g, b, 0, 0)),
            pl.BlockSpec((None, None, None, p, LANE), lambda b, g, i: (b, layer, g, 0, 0)),
            pl.BlockSpec((None, None, None, LANE, p), lambda b, g, i: (b, layer, g, 0, 0)),
            pl.BlockSpec((None, span, tq), lambda b, g, i: (variant(i), 0, 0)),
        ],
        out_specs=pl.BlockSpec((tq, qw), lambda b, g, i: (b * nb + i, g)),
        out_shape=jax.ShapeDtypeStruct((m, q_dim), BF16),
        scratch_shapes=[pltpu.SMEM((1,), F32)],
        compiler_params=_params("parallel", "parallel", "arbitrary"),
        name="window_attention",
    )(sink, qkv, qkv, v_t, k_ctx, v_ctx_t, bias)


def _dense_attn_kernel(q_ref, k_ref, vt_ref, kc_ref, vct_ref, o_ref, kmax_ref, *, r_heads, kv_chunk):
    t = k_ref.shape[0]

    @pl.when(pl.program_id(2) == 0)
    def _():
        kmax_ref[0] = _max_key_norm(k_ref, kc_ref)

    def segments():
        return [(kc_ref[...], vct_ref[...], None)] + [
            (k_ref[c * kv_chunk:(c + 1) * kv_chunk, :], vt_ref[:, c * kv_chunk:(c + 1) * kv_chunk], None)
            for c in range(t // kv_chunk)]

    _attend(q_ref, o_ref, segments, kmax_ref[0], None, r_heads)


def _dense_attention(qkv, v_t, k_ctx, v_ctx_t, layer, batch, q_dim, kv_dim):
    m = qkv.shape[0]
    t = m // batch
    n_kv = kv_dim // LANE
    r_heads = q_dim // kv_dim
    qw = r_heads * LANE
    tq = _tile(t, TILES["attn_q"], LANE)
    kv_chunk = _tile(t, TILES["attn_kv"], LANE)
    nb = t // tq
    p = k_ctx.shape[3]
    return pl.pallas_call(
        functools.partial(_dense_attn_kernel, r_heads=r_heads, kv_chunk=kv_chunk),
        grid=(batch, n_kv, nb),
        in_specs=[
            pl.BlockSpec((tq, qw), lambda b, g, i: (b * nb + i, g)),
            pl.BlockSpec((t, LANE), lambda b, g, i: (b, q_dim // LANE + g)),
            pl.BlockSpec((LANE, t), lambda b, g, i: (g, b)),
            pl.BlockSpec((None, None, None, p, LANE), lambda b, g, i: (b, layer, g, 0, 0)),
            pl.BlockSpec((None, None, None, LANE, p), lambda b, g, i: (b, layer, g, 0, 0)),
        ],
        out_specs=pl.BlockSpec((tq, qw), lambda b, g, i: (b * nb + i, g)),
        out_shape=jax.ShapeDtypeStruct((m, q_dim), BF16),
        scratch_shapes=[pltpu.SMEM((1,), F32)],
        compiler_params=_params("parallel", "parallel", "arbitrary"),
        name="dense_attention",
    )(qkv, qkv, v_t, k_ctx, v_ctx_t)


def _proj_res_kernel(a_ref, w_ref, x_ref, g_ref, o_ref):
    o_ref[...] = x_ref[...] + g_ref[...] * _dot(a_ref[...], w_ref[...])


def _proj_residual(a, w, layer, x, mod, gate_slot, bm_pref, bn_pref):
    m, k = a.shape
    d = w.shape[2]
    bm = _tile(min(m, mod.rows_per_group), bm_pref, PACK)
    bn = _tile(d, bn_pref, LANE)
    nj = d // bn
    gate = mod.spec(gate_slot, bm, bn, nj)
    return pl.pallas_call(
        _proj_res_kernel,
        grid=(nj, m // bm),
        in_specs=[
            pl.BlockSpec((bm, k), lambda j, i: (i, 0)),
            pl.BlockSpec((None, k, bn), lambda j, i: (layer, 0, j)),
            pl.BlockSpec((bm, bn), lambda j, i: (i, j)),
            pl.BlockSpec(gate.block_shape, lambda j, i: gate.index_map(i, j)),
        ],
        out_specs=pl.BlockSpec((bm, bn), lambda j, i: (i, j)),
        out_shape=jax.ShapeDtypeStruct((m, d), F32),
        compiler_params=_params("parallel", "parallel"),
        name="proj_residual",
    )(a, w, x, mod.table)


def _matmul_kernel(a_ref, w_ref, o_ref):
    o_ref[...] = _dot(a_ref[...], w_ref[...])


def _edge_rows_kernel(before_ref, after_ref, o_ref):
    row = lax.broadcasted_iota(jnp.int32, o_ref.shape, 0)
    last = before_ref[...].astype(F32)[PACK - 1:PACK, :]
    first = after_ref[...].astype(F32)[0:1, :]
    o_ref[...] = jnp.where(row == SUB - 1, last, jnp.where(row == SUB, first, 0.0)).astype(o_ref.dtype)


def _edge_gate(h, wg, layer, bm):
    m, d = h.shape
    nt = m // bm
    d_ff = wg.shape[2]
    per = bm // PACK
    edge = pl.pallas_call(
        _edge_rows_kernel,
        grid=(nt,),
        in_specs=[pl.BlockSpec((PACK, d), lambda i: (jnp.maximum(i * per - 1, 0), 0)),
                  pl.BlockSpec((PACK, d), lambda i: (jnp.minimum((i + 1) * per, nt * per - 1), 0))],
        out_specs=pl.BlockSpec((None, PACK, d), lambda i: (i, 0, 0)),
        out_shape=jax.ShapeDtypeStruct((nt, PACK, d), h.dtype),
        compiler_params=_params("parallel"),
        name="edge_rows",
    )(h, h)
    bn = min(d_ff, TILES["ffn_f"])
    out = pl.pallas_call(
        _matmul_kernel,
        grid=(pl.cdiv(d_ff, bn),),
        in_specs=[pl.BlockSpec((nt * PACK, d), lambda j: (0, 0)),
                  pl.BlockSpec((None, d, bn), lambda j: (layer, 0, j))],
        out_specs=pl.BlockSpec((nt * PACK, bn), lambda j: (0, j)),
        out_shape=jax.ShapeDtypeStruct((nt * PACK, d_ff), F32),
        compiler_params=_params("parallel"),
        name="edge_gate",
    )(edge.reshape(nt * PACK, d), wg)
    return out.reshape(nt, PACK, d_ff)


def _gate_up_kernel(*refs, bm, seq_len, n_cast):
    h_ref, wg_ref, wu_ref, cw_ref, cb_ref, e_ref = refs[:6]
    cast_in = refs[6:6 + n_cast]
    o_ref = refs[6 + n_cast]
    cast_out = refs[7 + n_cast:7 + 2 * n_cast]
    a_ref = refs[7 + 2 * n_cast]
    _cast_rows(cast_in, cast_out)
    a_ref[0:SUB, :] = e_ref[0:SUB, :]
    a_ref[SUB:SUB + bm, :] = _dot(h_ref[...], wg_ref[...])
    a_ref[SUB + bm:, :] = e_ref[SUB:, :]
    u = _dot(h_ref[...], wu_ref[...])
    row = pl.program_id(0) * bm + lax.broadcasted_iota(jnp.int32, (bm, 1), 0)
    pos = row % seq_len
    prev = jnp.where(pos == 0, 0.0, a_ref[SUB - 1:SUB - 1 + bm, :])
    nxt = jnp.where(pos == seq_len - 1, 0.0, a_ref[SUB + 1:SUB + 1 + bm, :])
    cur = a_ref[SUB:SUB + bm, :]
    a = prev * cw_ref[0:1, :] + cur * cw_ref[1:2, :] + nxt * cw_ref[2:3, :] + cb_ref[...]
    o_ref[...] = (_silu(a) * u).astype(o_ref.dtype)


def _gate_up(h, wg, wu, conv_w, conv_b, layer, seq_len, bm, casts=()):
    m, d = h.shape
    (wg_arr, wg_l), (wu_arr, wu_l) = wg, wu
    d_ff = wg_arr.shape[2]
    bf = min(d_ff, TILES["ffn_f"])
    n_f = pl.cdiv(d_ff, bf)
    steps = (m // bm) * n_f
    edge = _edge_gate(h, wg_arr, wg_l, bm)
    cast_in_specs, cast_out_specs, cast_shapes = _cast_specs(casts, steps, lambda i, f: i * n_f + f)
    out = pl.pallas_call(
        functools.partial(_gate_up_kernel, bm=bm, seq_len=seq_len, n_cast=len(casts)),
        grid=(m // bm, n_f),
        in_specs=[
            pl.BlockSpec((bm, d), lambda i, f: (i, 0)),
            pl.BlockSpec((None, d, bf), lambda i, f: (wg_l, 0, f)),
            pl.BlockSpec((None, d, bf), lambda i, f: (wu_l, 0, f)),
            pl.BlockSpec((None, conv_w.shape[1], bf), lambda i, f: (layer, 0, f)),
            pl.BlockSpec((None, 1, bf), lambda i, f: (layer, 0, f)),
            pl.BlockSpec((None, PACK, bf), lambda i, f: (i, 0, f)),
        ] + cast_in_specs,
        out_specs=[pl.BlockSpec((bm, bf), lambda i, f: (i, f))] + cast_out_specs,
        out_shape=[jax.ShapeDtypeStruct((m, d_ff), BF16)] + cast_shapes,
        scratch_shapes=[pltpu.VMEM((bm + PACK, bf), F32)],
        compiler_params=_params("arbitrary", "arbitrary"),
        name="gate_up",
    )(h, wg_arr, wu_arr, conv_w, conv_b, edge, *[arr for arr, _ in casts])
    return out[0], list(out[1:])


def _rope_tables(n_tokens):
    axis_dim = LANE // 2
    half = axis_dim // 2
    rows = n_tokens // GRID_W
    row = jnp.repeat(jnp.arange(rows, dtype=jnp.int32), GRID_W)
    col = jnp.tile(jnp.arange(GRID_W, dtype=jnp.int32), rows)
    inv_freq = ROPE_THETA ** (-jnp.arange(half, dtype=F32) / half)
    cos_parts, sin_parts = [], []
    for pos in (row, col):
        ang = pos.astype(F32)[:, None] * inv_freq[None, :]
        c, s = jnp.cos(ang), jnp.sin(ang)
        cos_parts += [c, c]
        sin_parts += [-s, s]
    return jnp.concatenate(cos_parts, axis=1), jnp.concatenate(sin_parts, axis=1)


def kernel(x_prompt, x_sample, cache_k, cache_v, c, c_ctx, w_mod, b_mod, norm_attn, norm_ffn, w_qkv, w_o,
           sink_a, q_norm_b, k_norm_b, w_gate, w_up, w_down, conv_w, conv_b, norm_f):
    batch_p, seq_p, d = x_prompt.shape
    batch_s, seq_s, _ = x_sample.shape
    depth = w_mod.shape[0]
    n_kv, head_dim = cache_k.shape[3], cache_k.shape[4]
    n_heads = sink_a.shape[1]
    assert head_dim == LANE
    q_dim, kv_dim = n_heads * head_dim, n_kv * head_dim
    d_ff = w_gate.shape[2]
    scale = head_dim ** -0.5
    m_p, m_s = batch_p * seq_p, batch_s * seq_s

    ffn_m_p = _tile(m_p, TILES["ffn_m"], seq_p)
    ffn_m_s = _tile(seq_s, TILES["ffn_m"], PACK)

    xp = x_prompt.reshape(m_p, d)
    xs = x_sample.reshape(m_s, d)

    n_cond = -(-(batch_s + 1) // SUB) * SUB
    cond = jnp.zeros((n_cond, d), F32).at[:batch_s].set(c).at[batch_s].set(c_ctx)
    mod_table = _modulation(cond, w_mod, b_mod).reshape(depth * n_cond, 1, -1)

    f32_weights = dict(qkv=w_qkv, o=w_o, gate=w_gate, up=w_up, down=w_down)
    bf16 = {("qkv", 0): (w_qkv[:1].astype(BF16), 0)}
    cw = conv_w
    cb = conv_b.reshape(depth, 1, d_ff)
    gains_attn = norm_attn.reshape(depth, 1, d)
    gains_ffn = norm_ffn.reshape(depth, 1, d)
    k_ctx = cache_k.transpose(0, 1, 3, 2, 4).astype(BF16)
    v_ctx_t = cache_v.transpose(0, 1, 3, 4, 2).astype(BF16)

    rope_tables = _rope_tables(seq_s)
    ones_kv = jnp.ones((kv_dim,), F32)
    ks_out, vs_out = [], []
    for i in range(depth):
        mod_s = _Mod(mod_table, i * n_cond, seq_s)
        mod_p = _Mod(mod_table, i * n_cond + batch_s, m_p)
        mixer_a = i % 2 == 0
        if mixer_a:
            sink = sink_a[i // 2].astype(F32) * LOG2E
            colscale = jnp.concatenate([jnp.full((q_dim,), scale * LOG2E, F32), ones_kv, ones_kv])
        else:
            sink = jnp.zeros((n_heads,), F32)
            colscale = jnp.concatenate([jnp.tile(q_norm_b[i // 2].astype(F32), n_heads) * (scale * LOG2E),
                                        jnp.tile(k_norm_b[i // 2].astype(F32), n_kv), ones_kv])
        colscale = colscale.reshape(1, -1)

        hp = _norm_mod(xp, gains_attn, i, mod_p, 0)
        hs = _norm_mod(xs, gains_attn, i, mod_s, 0)
        qkv_p, kv32 = _qkv_proj(hp, *bf16["qkv", i], colscale, q_dim, kv_dim, norm=not mixer_a)
        cast_names = ["o", "gate", "up"] if i == 0 else []
        qkv_s, cast = _qkv_proj_rope(hs, *bf16["qkv", i], colscale, q_dim, kv_dim, norm=not mixer_a,
                                     rope_tables=rope_tables,
                                     casts=[(f32_weights[name], i) for name in cast_names])
        for name, w in zip(cast_names, cast):
            bf16[name, i] = (w[None], 0)
        ks_out.append(kv32[:, :kv_dim].reshape(batch_p, seq_p, n_kv, head_dim))
        vs_out.append(kv32[:, kv_dim:].reshape(batch_p, seq_p, n_kv, head_dim))
        op = _self_attention(qkv_p, qkv_p[:, q_dim + kv_dim:].T, sink, batch_p, q_dim, kv_dim,
                             has_sink=mixer_a)
        v_s = qkv_s[:, q_dim + kv_dim:]
        if mixer_a:
            v_t = v_s.reshape(m_s // LANE, LANE, n_kv, LANE).transpose(2, 0, 3, 1)
            osm = _window_attention(qkv_s, v_t, k_ctx, v_ctx_t, i, sink, batch_s, q_dim, kv_dim)
        else:
            osm = _dense_attention(qkv_s, v_s.T, k_ctx, v_ctx_t, i, batch_s, q_dim, kv_dim)
        xp = _proj_residual(op, *bf16["o", i], xp, mod_p, 2, TILES["proj_m"], TILES["proj_n"])
        xs = _proj_residual(osm, *bf16["o", i], xs, mod_s, 2, TILES["proj_m"], TILES["proj_n"])

        hp = _norm_mod(xp, gains_ffn, i, mod_p, 3)
        hs = _norm_mod(xs, gains_ffn, i, mod_s, 3)
        cast_names = ["down"] + (["qkv", "o", "gate", "up"] if i + 1 < depth else [])
        zs, cast = _gate_up(hs, bf16["gate", i], bf16["up", i], cw, cb, i, seq_s, ffn_m_s,
                            casts=[(f32_weights[name], i if name == "down" else i + 1) for name in cast_names])
        for name, w in zip(cast_names, cast):
            bf16[name, i if name == "down" else i + 1] = (w[None], 0)
        zp, _ = _gate_up(hp, bf16["gate", i], bf16["up", i], cw, cb, i, seq_p, ffn_m_p)
        xp = _proj_residual(zp, *bf16["down", i], xp, mod_p, 5, TILES["down_m"], TILES["down_n"])
        xs = _proj_residual(zs, *bf16["down", i], xs, mod_s, 5, TILES["down_m"], TILES["down_n"])

    y_prompt = _final_norm(xp, norm_f).reshape(batch_p, seq_p, d)
    y_sample = _final_norm(xs, norm_f).reshape(batch_s, seq_s, d)
    return y_prompt, y_sample, jnp.stack(ks_out, axis=1), jnp.stack(vs_out, axis=1)
```

```python
import functools
import math

import jax
import jax.numpy as jnp
from jax import lax
from jax.experimental import pallas as pl
from jax.experimental.pallas import tpu as pltpu

GRID_W = 64
WINDOW = 128
ROPE_THETA = 10000.0
EPS = 1e-6
LANE = 128
SUB = 8
PACK = 16
NORM_ROWS = PACK
NORM_UNROLL = 8
NEG_BIG = -1e30
UNDERFLOW_SUM = 2.0 ** -60
LOG2E = math.log2(math.e)
VMEM_LIMIT_BYTES = 58 * 1024 * 1024

BF16 = jnp.bfloat16
F32 = jnp.float32

TILES = dict(
    mod_n=512,
    norm_m=512,
    qkv_m=1024, qkv_n=1024, qkv_rope_m=512,
    proj_m=1024, proj_n=1024,
    ffn_m=1024, ffn_f=512,
    down_m=512, down_n=512,
    win_q=256,
    attn_q=256, attn_kv=2048,
)


def _tile(dim, pref, mult):
    if dim <= pref:
        return dim
    t = pref - pref % mult
    while t >= mult:
        if dim % t == 0:
            return t
        t -= mult
    return dim


def _params(*semantics):
    return pltpu.CompilerParams(dimension_semantics=semantics, vmem_limit_bytes=VMEM_LIMIT_BYTES)


def _silu(x):
    return x / (1.0 + jnp.exp(-x))


def _dot(a, b):
    return jnp.dot(a, b, preferred_element_type=F32)


def _dot_nt(a, b):
    return lax.dot_general(a, b, (((1,), (1,)), ((), ())), preferred_element_type=F32)


def _mod_kernel(c_ref, w_ref, b_ref, o_ref):
    a = _silu(c_ref[...]).astype(BF16)
    o_ref[...] = _dot(a, w_ref[...].astype(BF16)) + b_ref[...]


def _modulation(cond, w_mod, b_mod):
    depth, d, n = w_mod.shape
    rows = cond.shape[0]
    tn = _tile(n, TILES["mod_n"], LANE)
    return pl.pallas_call(
        _mod_kernel,
        grid=(depth, n // tn),
        in_specs=[
            pl.BlockSpec((rows, d), lambda l, j: (0, 0)),
            pl.BlockSpec((None, d, tn), lambda l, j: (l, 0, j)),
            pl.BlockSpec((None, 1, tn), lambda l, j: (l, 0, j)),
        ],
        out_specs=pl.BlockSpec((None, rows, tn), lambda l, j: (l, 0, j)),
        out_shape=jax.ShapeDtypeStruct((depth, rows, n), F32),
        compiler_params=_params("parallel", "parallel"),
        name="modulation",
    )(cond, w_mod, b_mod.reshape(depth, 1, n))


class _Mod:
    def __init__(self, table, base, rows_per_group):
        self.table, self.base, self.rows_per_group = table, base, rows_per_group

    def spec(self, slot, bm, bn, nj):
        base, rpg = self.base, self.rows_per_group
        if nj == 1:
            return pl.BlockSpec((None, 1, bn), lambda i, *_: (base + (i * bm) // rpg, 0, slot))
        return pl.BlockSpec((None, 1, bn), lambda i, j: (base + (i * bm) // rpg, 0, slot * nj + j))


def _rms_rows(x_ref, o_ref, scale_ref, shift_ref):
    def body(r, carry):
        rows = pl.ds(pl.multiple_of(r * NORM_ROWS, NORM_ROWS), NORM_ROWS)
        x = x_ref[rows, :]
        inv = lax.rsqrt(jnp.mean(x * x, axis=-1, keepdims=True) + EPS)
        y = x_ref[rows, :] * inv * scale_ref[...]
        if shift_ref is not None:
            y = y + shift_ref[...]
        o_ref[rows, :] = y.astype(o_ref.dtype)
        return carry

    lax.fori_loop(0, x_ref.shape[0] // NORM_ROWS, body, 0, unroll=NORM_UNROLL)


def _norm_mod_kernel(x_ref, g_ref, sh_ref, sc_ref, o_ref, gs_ref):
    gs_ref[...] = g_ref[...] * (1.0 + sc_ref[...])
    _rms_rows(x_ref, o_ref, gs_ref, sh_ref)


def _norm_mod(x, gains, layer, mod, shift_slot):
    m, d = x.shape
    bm = _tile(min(m, mod.rows_per_group), TILES["norm_m"], NORM_ROWS)
    return pl.pallas_call(
        _norm_mod_kernel,
        grid=(m // bm,),
        in_specs=[
            pl.BlockSpec((bm, d), lambda i: (i, 0)),
            pl.BlockSpec((None, 1, d), lambda i: (layer, 0, 0)),
            mod.spec(shift_slot, bm, d, 1),
            mod.spec(shift_slot + 1, bm, d, 1),
        ],
        out_specs=pl.BlockSpec((bm, d), lambda i: (i, 0)),
        out_shape=jax.ShapeDtypeStruct((m, d), BF16),
        scratch_shapes=[pltpu.VMEM((1, d), F32)],
        compiler_params=_params("parallel"),
        name="norm_mod",
    )(x, gains, mod.table, mod.table)


def _norm_kernel(x_ref, g_ref, o_ref):
    _rms_rows(x_ref, o_ref, g_ref, None)


def _final_norm(x, gain):
    m, d = x.shape
    bm = _tile(m, TILES["norm_m"], NORM_ROWS)
    return pl.pallas_call(
        _norm_kernel,
        grid=(m // bm,),
        in_specs=[pl.BlockSpec((bm, d), lambda i: (i, 0)), pl.BlockSpec((1, d), lambda i: (0, 0))],
        out_specs=pl.BlockSpec((bm, d), lambda i: (i, 0)),
        out_shape=jax.ShapeDtypeStruct((m, d), F32),
        compiler_params=_params("parallel"),
        name="final_norm",
    )(x, gain.reshape(1, d))


def _qkv_kernel(h_ref, w_ref, cs_ref, o_ref, kv_ref, *, n_qk, norm):
    is_v = pl.program_id(1) >= n_qk
    acc = _dot(h_ref[...], w_ref[...])
    outs = []
    for hh in range(acc.shape[1] // LANE):
        blk = acc[:, hh * LANE:(hh + 1) * LANE]
        if norm:
            inv = lax.rsqrt(jnp.mean(blk * blk, axis=-1, keepdims=True) + EPS)
            blk = blk * jnp.where(is_v, 1.0, inv)
        outs.append(blk * cs_ref[:, hh * LANE:(hh + 1) * LANE])
    res = jnp.concatenate(outs, axis=1) if len(outs) > 1 else outs[0]
    o_ref[...] = res.astype(o_ref.dtype)
    kv_ref[...] = res


def _qkv_proj(h, w, layer, colscale, q_dim, kv_dim, *, norm):
    m, d = h.shape
    n = w.shape[2]
    bm = _tile(m, TILES["qkv_m"], PACK)
    bn = _tile(kv_dim, TILES["qkv_n"], LANE)
    assert q_dim % bn == 0 and kv_dim % bn == 0
    n_q = q_dim // bn
    return pl.pallas_call(
        functools.partial(_qkv_kernel, n_qk=(q_dim + kv_dim) // bn, norm=norm),
        grid=(m // bm, n // bn),
        in_specs=[
            pl.BlockSpec((bm, d), lambda i, j: (i, 0)),
            pl.BlockSpec((None, d, bn), lambda i, j: (layer, 0, j)),
            pl.BlockSpec((1, bn), lambda i, j: (0, j)),
        ],
        out_specs=[pl.BlockSpec((bm, bn), lambda i, j: (i, j)),
                   pl.BlockSpec((bm, bn), lambda i, j: (i, jnp.maximum(j - n_q, 0)))],
        out_shape=[jax.ShapeDtypeStruct((m, n), BF16), jax.ShapeDtypeStruct((m, 2 * kv_dim), F32)],
        compiler_params=_params("parallel", "arbitrary"),
        name="qkv_proj",
    )(h, w, colscale)


def _head_epilogue(acc, cs_ref, cos, sin, is_v, norm):
    lane = lax.broadcasted_iota(jnp.int32, (1, LANE), 1)
    first_half = (lane % (LANE // 2)) < (LANE // 4)
    outs = []
    for hh in range(acc.shape[1] // LANE):
        blk = acc[:, hh * LANE:(hh + 1) * LANE]
        if norm:
            inv = lax.rsqrt(jnp.mean(blk * blk, axis=-1, keepdims=True) + EPS)
            blk = blk * jnp.where(is_v, 1.0, inv)
        blk = blk * cs_ref[:, hh * LANE:(hh + 1) * LANE]
        partner = jnp.where(first_half, pltpu.roll(blk, LANE - LANE // 4, 1),
                            pltpu.roll(blk, LANE // 4, 1))
        outs.append(blk * cos + partner * sin)
    return jnp.concatenate(outs, axis=1) if len(outs) > 1 else outs[0]


def _cast_specs(casts, steps, step_of):
    in_specs, out_specs, shapes = [], [], []
    for arr, l in casts:
        _, r, c = arr.shape
        rows = PACK * pl.cdiv(r, PACK * steps)
        blk = lambda *ids, n=pl.cdiv(r, rows): jnp.minimum(step_of(*ids), n - 1)
        in_specs.append(pl.BlockSpec((None, rows, c), lambda *ids, l=l, blk=blk: (l, blk(*ids), 0)))
        out_specs.append(pl.BlockSpec((rows, c), lambda *ids, blk=blk: (blk(*ids), 0)))
        shapes.append(jax.ShapeDtypeStruct((r, c), BF16))
    return in_specs, out_specs, shapes


def _cast_rows(cast_in, cast_out):
    for src, dst in zip(cast_in, cast_out):
        dst[...] = src[...].astype(dst.dtype)


def _qkv_lag_kernel(*refs, n_i, n_qk, norm, n_cast):
    h_ref, w_ref, cs_ref, cos_ref, sin_ref = refs[:5]
    o_ref = refs[5 + n_cast]
    acc_ref = refs[6 + 2 * n_cast]
    _cast_rows(refs[5:5 + n_cast], refs[6 + n_cast:6 + 2 * n_cast])
    s = pl.program_id(0)

    @pl.when(s == 0)
    def _():
        acc_ref[...] = jnp.zeros_like(acc_ref)

    is_v = jnp.maximum(s - 1, 0) // n_i >= n_qk
    cos = jnp.where(is_v, 1.0, cos_ref[...])
    sin = jnp.where(is_v, 0.0, sin_ref[...])
    o_ref[...] = _head_epilogue(acc_ref[...], cs_ref, cos, sin, is_v, norm).astype(o_ref.dtype)
    acc_ref[...] = _dot(h_ref[...], w_ref[...])


def _qkv_proj_rope(h, w, layer, colscale, q_dim, kv_dim, *, norm, rope_tables, casts=()):
    m, d = h.shape
    n = w.shape[2]
    bm = _tile(m, TILES["qkv_rope_m"], PACK)
    bn = _tile(kv_dim, TILES["qkv_n"], LANE)
    assert q_dim % bn == 0 and kv_dim % bn == 0
    n_i = m // bm
    steps = n_i * (n // bn)
    cos, sin = rope_tables
    per = cos.shape[0] // bm
    assert cos.shape[0] % bm == 0
    cur = lambda s: jnp.minimum(s, steps - 1)
    fin = lambda s: jnp.maximum(s - 1, 0)
    cast_in, cast_out, cast_shapes = _cast_specs(casts, steps + 1, lambda s: s)
    out = pl.pallas_call(
        functools.partial(_qkv_lag_kernel, n_i=n_i, n_qk=(q_dim + kv_dim) // bn, norm=norm,
                          n_cast=len(casts)),
        grid=(steps + 1,),
        in_specs=[
            pl.BlockSpec((bm, d), lambda s: (cur(s) % n_i, 0)),
            pl.BlockSpec((None, d, bn), lambda s: (layer, 0, cur(s) // n_i)),
            pl.BlockSpec((1, bn), lambda s: (0, fin(s) // n_i)),
            pl.BlockSpec((bm, LANE), lambda s: ((fin(s) % n_i) % per, 0)),
            pl.BlockSpec((bm, LANE), lambda s: ((fin(s) % n_i) % per, 0)),
        ] + cast_in,
        out_specs=[pl.BlockSpec((bm, bn), lambda s: (fin(s) % n_i, fin(s) // n_i))] + cast_out,
        out_shape=[jax.ShapeDtypeStruct((m, n), BF16)] + cast_shapes,
        scratch_shapes=[pltpu.VMEM((bm, bn), F32)],
        compiler_params=_params("arbitrary"),
        name="qkv_proj_rope",
    )(h, w, colscale, cos, sin, *[arr for arr, _ in casts])
    return out[0], list(out[1:])


def _stack_heads(q_ref, r_heads):
    return jnp.concatenate([q_ref[:, r * LANE:(r + 1) * LANE] for r in range(r_heads)], axis=0)


def _store_heads_t(o_ref, o_t, r_heads):
    tq = o_t.shape[1] // r_heads
    for r in range(r_heads):
        o_ref[:, r * LANE:(r + 1) * LANE] = o_t[:, r * tq:(r + 1) * tq].T.astype(o_ref.dtype)


def _max_key_norm(*k_refs):
    def max_sq_norm(ref):
        x = ref[...].astype(F32)
        return jnp.max(jnp.sum(x * x, axis=1, keepdims=True))
    return jnp.sqrt(functools.reduce(jnp.maximum, [max_sq_norm(r) for r in k_refs]))


def _sink_row(sink_ref, g, r_heads, tq):
    return jnp.concatenate(
        [jnp.full((1, tq), sink_ref[g * r_heads + r], F32) for r in range(r_heads)], axis=1)


def _attend(q_ref, o_ref, segments, k_max, sink, r_heads):
    qs = _stack_heads(q_ref, r_heads)
    rows = qs.shape[0]
    q_sq = _dot_nt(jnp.ones((SUB, LANE), BF16), qs * qs)[0:1]
    bound = jnp.sqrt(q_sq) * k_max
    if sink is not None:
        bound = jnp.maximum(bound, sink)

    def scores(k, bias):
        s = _dot_nt(k, qs)
        return s if bias is None else s + bias

    def attempt(off):
        den = jnp.zeros((1, rows), F32)
        acc = jnp.zeros((LANE, rows), F32)
        for k, v_t, bias in segments():
            p = jnp.exp2(scores(k, bias) - off)
            den = den + jnp.sum(p, axis=0, keepdims=True)
            acc = acc + _dot(v_t, p.astype(BF16))
        if sink is not None:
            den = den + jnp.exp2(sink - off)
        return acc * (1.0 / den), den

    out, den = attempt(bound)
    _store_heads_t(o_ref, out, r_heads)

    @pl.when(jnp.min(den) < UNDERFLOW_SUM)
    def _():
        row_max = jnp.full((1, rows), NEG_BIG, F32) if sink is None else sink
        for k, _, bias in segments():
            row_max = jnp.maximum(row_max, jnp.max(scores(k, bias), axis=0, keepdims=True))
        out, _ = attempt(row_max)
        _store_heads_t(o_ref, out, r_heads)


def _self_attn_kernel(sink_ref, q_ref, k_ref, vt_ref, o_ref, kmax_ref, *, r_heads, n_kv, has_sink):
    t = q_ref.shape[0]
    qw = r_heads * LANE
    for g in range(n_kv):
        k_g = k_ref.at[:, g * LANE:(g + 1) * LANE]
        vt_g = vt_ref.at[g * LANE:(g + 1) * LANE, :]
        kmax_ref[g] = _max_key_norm(k_g)
        sink = _sink_row(sink_ref, g, r_heads, t) if has_sink else None
        _attend(q_ref.at[:, g * qw:(g + 1) * qw], o_ref.at[:, g * qw:(g + 1) * qw],
                lambda k_g=k_g, vt_g=vt_g: [(k_g[...], vt_g[...], None)], kmax_ref[g], sink, r_heads)


def _self_attention(qkv, v_t, sink, batch, q_dim, kv_dim, *, has_sink):
    m = qkv.shape[0]
    t = m // batch
    n_kv = kv_dim // LANE
    r_heads = q_dim // kv_dim
    return pl.pallas_call(
        functools.partial(_self_attn_kernel, r_heads=r_heads, n_kv=n_kv, has_sink=has_sink),
        grid=(batch,),
        in_specs=[
            pl.BlockSpec(memory_space=pltpu.SMEM),
            pl.BlockSpec((t, q_dim), lambda b: (b, 0)),
            pl.BlockSpec((t, kv_dim), lambda b: (b, q_dim // kv_dim)),
            pl.BlockSpec((kv_dim, t), lambda b: (0, b)),
        ],
        out_specs=pl.BlockSpec((t, q_dim), lambda b: (b, 0)),
        out_shape=jax.ShapeDtypeStruct((m, q_dim), BF16),
        scratch_shapes=[pltpu.SMEM((n_kv,), F32)],
        compiler_params=_params("parallel"),
        name="self_attention",
    )(sink, qkv, qkv, v_t)


def _win_attn_kernel(sink_ref, q_ref, k_ref, vt_ref, kc_ref, vct_ref, bias_ref, o_ref,
                     kmax_ref, *, r_heads, span):
    i = pl.program_id(2)
    tq = q_ref.shape[0]
    t = k_ref.shape[0]

    @pl.when(i == 0)
    def _():
        kmax_ref[0] = _max_key_norm(k_ref, kc_ref)

    base = pl.multiple_of(jnp.clip(i * tq - WINDOW, 0, t - span), LANE)

    def segments():
        k_w = k_ref[pl.ds(base, span), :]
        v_w_t = jnp.concatenate([vt_ref[base // LANE + c] for c in range(span // LANE)], axis=1)
        bias = jnp.concatenate([bias_ref[...]] * r_heads, axis=1)
        return [(kc_ref[...], vct_ref[...], None), (k_w, v_w_t, bias)]

    _attend(q_ref, o_ref, segments, kmax_ref[0],
            _sink_row(sink_ref, pl.program_id(1), r_heads, tq), r_heads)


def _window_attention(qkv, v_t, k_ctx, v_ctx_t, layer, sink, batch, q_dim, kv_dim):
    m = qkv.shape[0]
    t = m // batch
    n_kv = kv_dim // LANE
    r_heads = q_dim // kv_dim
    qw = r_heads * LANE
    tq = _tile(t, TILES["win_q"], LANE)
    reach = -(-WINDOW // LANE) * LANE
    span = tq + 2 * reach
    assert t % tq == 0 and t >= span
    nb = t // tq
    p = k_ctx.shape[3]
    blk = jnp.array([0, min(1, nb - 1), nb - 1], dtype=jnp.int32)[:, None, None]
    base = jnp.clip(blk * tq - reach, 0, t - span)
    kpos = base + jnp.arange(span, dtype=jnp.int32)[None, :, None]
    qpos = blk * tq + jnp.arange(tq, dtype=jnp.int32)[None, None, :]
    bias = jnp.where(jnp.abs(qpos - kpos) <= WINDOW, 0.0, NEG_BIG).astype(F32)
    variant = lambda i: jnp.where(i == 0, 0, jnp.where(i == nb - 1, 2, 1))
    return pl.pallas_call(
        functools.partial(_win_attn_kernel, r_heads=r_heads, span=span),
        grid=(batch, n_kv, nb),
        in_specs=[
            pl.BlockSpec(memory_space=pltpu.SMEM),
            pl.BlockSpec((tq, qw), lambda b, g, i: (b * nb + i, g)),
            pl.BlockSpec((t, LANE), lambda b, g, i: (b, q_dim // LANE + g)),
            pl.BlockSpec((None, t // LANE, LANE, LANE), lambda b, g, i: (g, b, 0, 0)),
            pl.BlockSpec((None, None, None, p, LANE), lambda b, g, i: (b, layer, g, 0, 0)),
            pl.BlockSpec((None, None, None, LANE, p), lambda b, g, i: (b, layer, g, 0, 0)),
            pl.BlockSpec((None, span, tq), lambda b, g, i: (variant(i), 0, 0)),
        ],
        out_specs=pl.BlockSpec((tq, qw), lambda b, g, i: (b * nb + i, g)),
        out_shape=jax.ShapeDtypeStruct((m, q_dim), BF16),
        scratch_shapes=[pltpu.SMEM((1,), F32)],
        compiler_params=_params("parallel", "parallel", "arbitrary"),
        name="window_attention",
    )(sink, qkv, qkv, v_t, k_ctx, v_ctx_t, bias)


def _dense_attn_kernel(q_ref, k_ref, vt_ref, kc_ref, vct_ref, o_ref, kmax_ref, *, r_heads, kv_chunk):
    t = k_ref.shape[0]

    @pl.when(pl.program_id(2) == 0)
    def _():
        kmax_ref[0] = _max_key_norm(k_ref, kc_ref)

    def segments():
        return [(kc_ref[...], vct_ref[...], None)] + [
            (k_ref[c * kv_chunk:(c + 1) * kv_chunk, :], vt_ref[:, c * kv_chunk:(c + 1) * kv_chunk], None)
            for c in range(t // kv_chunk)]

    _attend(q_ref, o_ref, segments, kmax_ref[0], None, r_heads)


def _dense_attention(qkv, v_t, k_ctx, v_ctx_t, layer, batch, q_dim, kv_dim):
    m = qkv.shape[0]
    t = m // batch
    n_kv = kv_dim // LANE
    r_heads = q_dim // kv_dim
    qw = r_heads * LANE
    tq = _tile(t, TILES["attn_q"], LANE)
    kv_chunk = _tile(t, TILES["attn_kv"], LANE)
    nb = t // tq
    p = k_ctx.shape[3]
    return pl.pallas_call(
        functools.partial(_dense_attn_kernel, r_heads=r_heads, kv_chunk=kv_chunk),
        grid=(batch, n_kv, nb),
        in_specs=[
            pl.BlockSpec((tq, qw), lambda b, g, i: (b * nb + i, g)),
            pl.BlockSpec((t, LANE), lambda b, g, i: (b, q_dim // LANE + g)),
            pl.BlockSpec((LANE, t), lambda b, g, i: (g, b)),
            pl.BlockSpec((None, None, None, p, LANE), lambda b, g, i: (b, layer, g, 0, 0)),
            pl.BlockSpec((None, None, None, LANE, p), lambda b, g, i: (b, layer, g, 0, 0)),
        ],
        out_specs=pl.BlockSpec((tq, qw), lambda b, g, i: (b * nb + i, g)),
        out_shape=jax.ShapeDtypeStruct((m, q_dim), BF16),
        scratch_shapes=[pltpu.SMEM((1,), F32)],
        compiler_params=_params("parallel", "parallel", "arbitrary"),
        name="dense_attention",
    )(qkv, qkv, v_t, k_ctx, v_ctx_t)


def _proj_res_kernel(a_ref, w_ref, x_ref, g_ref, o_ref):
    o_ref[...] = x_ref[...] + g_ref[...] * _dot(a_ref[...], w_ref[...])


def _proj_residual(a, w, layer, x, mod, gate_slot, bm_pref, bn_pref):
    m, k = a.shape
    d = w.shape[2]
    bm = _tile(min(m, mod.rows_per_group), bm_pref, PACK)
    bn = _tile(d, bn_pref, LANE)
    nj = d // bn
    gate = mod.spec(gate_slot, bm, bn, nj)
    return pl.pallas_call(
        _proj_res_kernel,
        grid=(nj, m // bm),
        in_specs=[
            pl.BlockSpec((bm, k), lambda j, i: (i, 0)),
            pl.BlockSpec((None, k, bn), lambda j, i: (layer, 0, j)),
            pl.BlockSpec((bm, bn), lambda j, i: (i, j)),
            pl.BlockSpec(gate.block_shape, lambda j, i: gate.index_map(i, j)),
        ],
        out_specs=pl.BlockSpec((bm, bn), lambda j, i: (i, j)),
        out_shape=jax.ShapeDtypeStruct((m, d), F32),
        compiler_params=_params("parallel", "parallel"),
        name="proj_residual",
    )(a, w, x, mod.table)


def _matmul_kernel(a_ref, w_ref, o_ref):
    o_ref[...] = _dot(a_ref[...], w_ref[...])


def _edge_rows_kernel(before_ref, after_ref, o_ref):
    row = lax.broadcasted_iota(jnp.int32, o_ref.shape, 0)
    last = before_ref[...].astype(F32)[PACK - 1:PACK, :]
    first = after_ref[...].astype(F32)[0:1, :]
    o_ref[...] = jnp.where(row == SUB - 1, last, jnp.where(row == SUB, first, 0.0)).astype(o_ref.dtype)


def _edge_gate(h, wg, layer, bm):
    m, d = h.shape
    nt = m // bm
    d_ff = wg.shape[2]
    per = bm // PACK
    edge = pl.pallas_call(
        _edge_rows_kernel,
        grid=(nt,),
        in_specs=[pl.BlockSpec((PACK, d), lambda i: (jnp.maximum(i * per - 1, 0), 0)),
                  pl.BlockSpec((PACK, d), lambda i: (jnp.minimum((i + 1) * per, nt * per - 1), 0))],
        out_specs=pl.BlockSpec((None, PACK, d), lambda i: (i, 0, 0)),
        out_shape=jax.ShapeDtypeStruct((nt, PACK, d), h.dtype),
        compiler_params=_params("parallel"),
        name="edge_rows",
    )(h, h)
    bn = min(d_ff, TILES["ffn_f"])
    out = pl.pallas_call(
        _matmul_kernel,
        grid=(pl.cdiv(d_ff, bn),),
        in_specs=[pl.BlockSpec((nt * PACK, d), lambda j: (0, 0)),
                  pl.BlockSpec((None, d, bn), lambda j: (layer, 0, j))],
        out_specs=pl.BlockSpec((nt * PACK, bn), lambda j: (0, j)),
        out_shape=jax.ShapeDtypeStruct((nt * PACK, d_ff), F32),
        compiler_params=_params("parallel"),
        name="edge_gate",
    )(edge.reshape(nt * PACK, d), wg)
    return out.reshape(nt, PACK, d_ff)


def _gate_up_kernel(*refs, bm, seq_len, n_cast):
    h_ref, wg_ref, wu_ref, cw_ref, cb_ref, e_ref = refs[:6]
    cast_in = refs[6:6 + n_cast]
    o_ref = refs[6 + n_cast]
    cast_out = refs[7 + n_cast:7 + 2 * n_cast]
    a_ref = refs[7 + 2 * n_cast]
    _cast_rows(cast_in, cast_out)
    a_ref[0:SUB, :] = e_ref[0:SUB, :]
    a_ref[SUB:SUB + bm, :] = _dot(h_ref[...], wg_ref[...])
    a_ref[SUB + bm:, :] = e_ref[SUB:, :]
    u = _dot(h_ref[...], wu_ref[...])
    row = pl.program_id(0) * bm + lax.broadcasted_iota(jnp.int32, (bm, 1), 0)
    pos = row % seq_len
    prev = jnp.where(pos == 0, 0.0, a_ref[SUB - 1:SUB - 1 + bm, :])
    nxt = jnp.where(pos == seq_len - 1, 0.0, a_ref[SUB + 1:SUB + 1 + bm, :])
    cur = a_ref[SUB:SUB + bm, :]
    a = prev * cw_ref[0:1, :] + cur * cw_ref[1:2, :] + nxt * cw_ref[2:3, :] + cb_ref[...]
    o_ref[...] = (_silu(a) * u).astype(o_ref.dtype)


def _gate_up(h, wg, wu, conv_w, conv_b, layer, seq_len, bm, casts=()):
    m, d = h.shape
    (wg_arr, wg_l), (wu_arr, wu_l) = wg, wu
    d_ff = wg_arr.shape[2]
    bf = min(d_ff, TILES["ffn_f"])
    n_f = pl.cdiv(d_ff, bf)
    steps = (m // bm) * n_f
    edge = _edge_gate(h, wg_arr, wg_l, bm)
    cast_in_specs, cast_out_specs, cast_shapes = _cast_specs(casts, steps, lambda i, f: i * n_f + f)
    out = pl.pallas_call(
        functools.partial(_gate_up_kernel, bm=bm, seq_len=seq_len, n_cast=len(casts)),
        grid=(m // bm, n_f),
        in_specs=[
            pl.BlockSpec((bm, d), lambda i, f: (i, 0)),
            pl.BlockSpec((None, d, bf), lambda i, f: (wg_l, 0, f)),
            pl.BlockSpec((None, d, bf), lambda i, f: (wu_l, 0, f)),
            pl.BlockSpec((None, conv_w.shape[1], bf), lambda i, f: (layer, 0, f)),
            pl.BlockSpec((None, 1, bf), lambda i, f: (layer, 0, f)),
            pl.BlockSpec((None, PACK, bf), lambda i, f: (i, 0, f)),
        ] + cast_in_specs,
        out_specs=[pl.BlockSpec((bm, bf), lambda i, f: (i, f))] + cast_out_specs,
        out_shape=[jax.ShapeDtypeStruct((m, d_ff), BF16)] + cast_shapes,
        scratch_shapes=[pltpu.VMEM((bm + PACK, bf), F32)],
        compiler_params=_params("arbitrary", "arbitrary"),
        name="gate_up",
    )(h, wg_arr, wu_arr, conv_w, conv_b, edge, *[arr for arr, _ in casts])
    return out[0], list(out[1:])


def _rope_tables(n_tokens):
    axis_dim = LANE // 2
    half = axis_dim // 2
    rows = n_tokens // GRID_W
    row = jnp.repeat(jnp.arange(rows, dtype=jnp.int32), GRID_W)
    col = jnp.tile(jnp.arange(GRID_W, dtype=jnp.int32), rows)
    inv_freq = ROPE_THETA ** (-jnp.arange(half, dtype=F32) / half)
    cos_parts, sin_parts = [], []
    for pos in (row, col):
        ang = pos.astype(F32)[:, None] * inv_freq[None, :]
        c, s = jnp.cos(ang), jnp.sin(ang)
        cos_parts += [c, c]
        sin_parts += [-s, s]
    return jnp.concatenate(cos_parts, axis=1), jnp.concatenate(sin_parts, axis=1)


def kernel(x_prompt, x_sample, cache_k, cache_v, c, c_ctx, w_mod, b_mod, norm_attn, norm_ffn, w_qkv, w_o,
           sink_a, q_norm_b, k_norm_b, w_gate, w_up, w_down, conv_w, conv_b, norm_f):
    batch_p, seq_p, d = x_prompt.shape
    batch_s, seq_s, _ = x_sample.shape
    depth = w_mod.shape[0]
    n_kv, head_dim = cache_k.shape[3], cache_k.shape[4]
    n_heads = sink_a.shape[1]
    assert head_dim == LANE
    q_dim, kv_dim = n_heads * head_dim, n_kv * head_dim
    d_ff = w_gate.shape[2]
    scale = head_dim ** -0.5
    m_p, m_s = batch_p * seq_p, batch_s * seq_s

    ffn_m_p = _tile(m_p, TILES["ffn_m"], seq_p)
    ffn_m_s = _tile(seq_s, TILES["ffn_m"], PACK)

    xp = x_prompt.reshape(m_p, d)
    xs = x_sample.reshape(m_s, d)

    n_cond = -(-(batch_s + 1) // SUB) * SUB
    cond = jnp.zeros((n_cond, d), F32).at[:batch_s].set(c).at[batch_s].set(c_ctx)
    mod_table = _modulation(cond, w_mod, b_mod).reshape(depth * n_cond, 1, -1)

    f32_weights = dict(qkv=w_qkv, o=w_o, gate=w_gate, up=w_up, down=w_down)
    bf16 = {("qkv", 0): (w_qkv[:1].astype(BF16), 0)}
    cw = conv_w
    cb = conv_b.reshape(depth, 1, d_ff)
    gains_attn = norm_attn.reshape(depth, 1, d)
    gains_ffn = norm_ffn.reshape(depth, 1, d)
    k_ctx = cache_k.transpose(0, 1, 3, 2, 4).astype(BF16)
    v_ctx_t = cache_v.transpose(0, 1, 3, 4, 2).astype(BF16)

    rope_tables = _rope_tables(seq_s)
    ones_kv = jnp.ones((kv_dim,), F32)
    ks_out, vs_out = [], []
    for i in range(depth):
        mod_s = _Mod(mod_table, i * n_cond, seq_s)
        mod_p = _Mod(mod_table, i * n_cond + batch_s, m_p)
        mixer_a = i % 2 == 0
        if mixer_a:
            sink = sink_a[i // 2].astype(F32) * LOG2E
            colscale = jnp.concatenate([jnp.full((q_dim,), scale * LOG2E, F32), ones_kv, ones_kv])
        else:
            sink = jnp.zeros((n_heads,), F32)
            colscale = jnp.concatenate([jnp.tile(q_norm_b[i // 2].astype(F32), n_heads) * (scale * LOG2E),
                                        jnp.tile(k_norm_b[i // 2].astype(F32), n_kv), ones_kv])
        colscale = colscale.reshape(1, -1)

        hp = _norm_mod(xp, gains_attn, i, mod_p, 0)
        hs = _norm_mod(xs, gains_attn, i, mod_s, 0)
        qkv_p, kv32 = _qkv_proj(hp, *bf16["qkv", i], colscale, q_dim, kv_dim, norm=not mixer_a)
        cast_names = ["o", "gate", "up"] if i == 0 else []
        qkv_s, cast = _qkv_proj_rope(hs, *bf16["qkv", i], colscale, q_dim, kv_dim, norm=not mixer_a,
                                     rope_tables=rope_tables,
                                     casts=[(f32_weights[name], i) for name in cast_names])
        for name, w in zip(cast_names, cast):
            bf16[name, i] = (w[None], 0)
        ks_out.append(kv32[:, :kv_dim].reshape(batch_p, seq_p, n_kv, head_dim))
        vs_out.append(kv32[:, kv_dim:].reshape(batch_p, seq_p, n_kv, head_dim))
        op = _self_attention(qkv_p, qkv_p[:, q_dim + kv_dim:].T, sink, batch_p, q_dim, kv_dim,
                             has_sink=mixer_a)
        v_s = qkv_s[:, q_dim + kv_dim:]
        if mixer_a:
            v_t = v_s.reshape(m_s // LANE, LANE, n_kv, LANE).transpose(2, 0, 3, 1)
            osm = _window_attention(qkv_s, v_t, k_ctx, v_ctx_t, i, sink, batch_s, q_dim, kv_dim)
        else:
            osm = _dense_attention(qkv_s, v_s.T, k_ctx, v_ctx_t, i, batch_s, q_dim, kv_dim)
        xp = _proj_residual(op, *bf16["o", i], xp, mod_p, 2, TILES["proj_m"], TILES["proj_n"])
        xs = _proj_residual(osm, *bf16["o", i], xs, mod_s, 2, TILES["proj_m"], TILES["proj_n"])

        hp = _norm_mod(xp, gains_ffn, i, mod_p, 3)
        hs = _norm_mod(xs, gains_ffn, i, mod_s, 3)
        cast_names = ["down"] + (["qkv", "o", "gate", "up"] if i + 1 < depth else [])
        zs, cast = _gate_up(hs, bf16["gate", i], bf16["up", i], cw, cb, i, seq_s, ffn_m_s,
                            casts=[(f32_weights[name], i if name == "down" else i + 1) for name in cast_names])
        for name, w in zip(cast_names, cast):
            bf16[name, i if name == "down" else i + 1] = (w[None], 0)
        zp, _ = _gate_up(hp, bf16["gate", i], bf16["up", i], cw, cb, i, seq_p, ffn_m_p)
        xp = _proj_residual(zp, *bf16["down", i], xp, mod_p, 5, TILES["down_m"], TILES["down_n"])
        xs = _proj_residual(zs, *bf16["down", i], xs, mod_s, 5, TILES["down_m"], TILES["down_n"])

    y_prompt = _final_norm(xp, norm_f).reshape(batch_p, seq_p, d)
    y_sample = _final_norm(xs, norm_f).reshape(batch_s, seq_s, d)
    return y_prompt, y_sample, jnp.stack(ks_out, axis=1), jnp.stack(vs_out, axis=1)
```

```python
import functools
import math

import jax
import jax.numpy as jnp
from jax import lax
from jax.experimental import pallas as pl
from jax.experimental.pallas import tpu as pltpu

GRID_W = 64
WINDOW = 128
ROPE_THETA = 10000.0
EPS = 1e-6
LANE = 128
SUB = 8
PACK = 16
NORM_ROWS = PACK
NORM_UNROLL = 8
NEG_BIG = -1e30
UNDERFLOW_SUM = 2.0 ** -60
LOG2E = math.log2(math.e)
VMEM_LIMIT_BYTES = 58 * 1024 * 1024

BF16 = jnp.bfloat16
F32 = jnp.float32

TILES = dict(
    mod_n=512,
    norm_m=512,
    qkv_m=1024, qkv_n=1024, qkv_rope_m=1024,
    proj_m=1024, proj_n=1024,
    ffn_m=1024, ffn_f=512,
    down_m=512, down_n=512,
    win_q=256,
    attn_q=256, attn_kv=2048,
)


def _tile(dim, pref, mult):
    if dim <= pref:
        return dim
    t = pref - pref % mult
    while t >= mult:
        if dim % t == 0:
            return t
        t -= mult
    return dim


def _params(*semantics):
    return pltpu.CompilerParams(dimension_semantics=semantics, vmem_limit_bytes=VMEM_LIMIT_BYTES)


def _silu(x):
    return x / (1.0 + jnp.exp(-x))


def _dot(a, b):
    return jnp.dot(a, b, preferred_element_type=F32)


def _dot_nt(a, b):
    return lax.dot_general(a, b, (((1,), (1,)), ((), ())), preferred_element_type=F32)


def _mod_kernel(c_ref, w_ref, b_ref, o_ref):
    a = _silu(c_ref[...]).astype(BF16)
    o_ref[...] = _dot(a, w_ref[...].astype(BF16)) + b_ref[...]


def _modulation(cond, w_mod, b_mod):
    depth, d, n = w_mod.shape
    rows = cond.shape[0]
    tn = _tile(n, TILES["mod_n"], LANE)
    return pl.pallas_call(
        _mod_kernel,
        grid=(depth, n // tn),
        in_specs=[
            pl.BlockSpec((rows, d), lambda l, j: (0, 0)),
            pl.BlockSpec((None, d, tn), lambda l, j: (l, 0, j)),
            pl.BlockSpec((None, 1, tn), lambda l, j: (l, 0, j)),
        ],
        out_specs=pl.BlockSpec((None, rows, tn), lambda l, j: (l, 0, j)),
        out_shape=jax.ShapeDtypeStruct((depth, rows, n), F32),
        compiler_params=_params("parallel", "parallel"),
        name="modulation",
    )(cond, w_mod, b_mod.reshape(depth, 1, n))


class _Mod:
    def __init__(self, table, base, rows_per_group):
        self.table, self.base, self.rows_per_group = table, base, rows_per_group

    def spec(self, slot, bm, bn, nj):
        base, rpg = self.base, self.rows_per_group
        if nj == 1:
            return pl.BlockSpec((None, 1, bn), lambda i, *_: (base + (i * bm) // rpg, 0, slot))
        return pl.BlockSpec((None, 1, bn), lambda i, j: (base + (i * bm) // rpg, 0, slot * nj + j))


def _rms_rows(x_ref, o_ref, scale_ref, shift_ref):
    def body(r, carry):
        rows = pl.ds(pl.multiple_of(r * NORM_ROWS, NORM_ROWS), NORM_ROWS)
        x = x_ref[rows, :]
        inv = lax.rsqrt(jnp.mean(x * x, axis=-1, keepdims=True) + EPS)
        y = x_ref[rows, :] * inv * scale_ref[...]
        if shift_ref is not None:
            y = y + shift_ref[...]
        o_ref[rows, :] = y.astype(o_ref.dtype)
        return carry

    lax.fori_loop(0, x_ref.shape[0] // NORM_ROWS, body, 0, unroll=NORM_UNROLL)


def _norm_mod_kernel(x_ref, g_ref, sh_ref, sc_ref, o_ref, gs_ref):
    gs_ref[...] = g_ref[...] * (1.0 + sc_ref[...])
    _rms_rows(x_ref, o_ref, gs_ref, sh_ref)


def _norm_mod(x, gains, layer, mod, shift_slot):
    m, d = x.shape
    bm = _tile(min(m, mod.rows_per_group), TILES["norm_m"], NORM_ROWS)
    return pl.pallas_call(
        _norm_mod_kernel,
        grid=(m // bm,),
        in_specs=[
            pl.BlockSpec((bm, d), lambda i: (i, 0)),
            pl.BlockSpec((None, 1, d), lambda i: (layer, 0, 0)),
            mod.spec(shift_slot, bm, d, 1),
            mod.spec(shift_slot + 1, bm, d, 1),
        ],
        out_specs=pl.BlockSpec((bm, d), lambda i: (i, 0)),
        out_shape=jax.ShapeDtypeStruct((m, d), BF16),
        scratch_shapes=[pltpu.VMEM((1, d), F32)],
        compiler_params=_params("parallel"),
        name="norm_mod",
    )(x, gains, mod.table, mod.table)


def _norm_kernel(x_ref, g_ref, o_ref):
    _rms_rows(x_ref, o_ref, g_ref, None)


def _final_norm(x, gain):
    m, d = x.shape
    bm = _tile(m, TILES["norm_m"], NORM_ROWS)
    return pl.pallas_call(
        _norm_kernel,
        grid=(m // bm,),
        in_specs=[pl.BlockSpec((bm, d), lambda i: (i, 0)), pl.BlockSpec((1, d), lambda i: (0, 0))],
        out_specs=pl.BlockSpec((bm, d), lambda i: (i, 0)),
        out_shape=jax.ShapeDtypeStruct((m, d), F32),
        compiler_params=_params("parallel"),
        name="final_norm",
    )(x, gain.reshape(1, d))


def _qkv_kernel(h_ref, w_ref, cs_ref, o_ref, kv_ref, *, n_qk, norm):
    is_v = pl.program_id(1) >= n_qk
    acc = _dot(h_ref[...], w_ref[...])
    outs = []
    for hh in range(acc.shape[1] // LANE):
        blk = acc[:, hh * LANE:(hh + 1) * LANE]
        if norm:
            inv = lax.rsqrt(jnp.mean(blk * blk, axis=-1, keepdims=True) + EPS)
            blk = blk * jnp.where(is_v, 1.0, inv)
        outs.append(blk * cs_ref[:, hh * LANE:(hh + 1) * LANE])
    res = jnp.concatenate(outs, axis=1) if len(outs) > 1 else outs[0]
    o_ref[...] = res.astype(o_ref.dtype)
    kv_ref[...] = res


def _qkv_proj(h, w, layer, colscale, q_dim, kv_dim, *, norm):
    m, d = h.shape
    n = w.shape[2]
    bm = _tile(m, TILES["qkv_m"], PACK)
    bn = _tile(kv_dim, TILES["qkv_n"], LANE)
    assert q_dim % bn == 0 and kv_dim % bn == 0
    n_q = q_dim // bn
    return pl.pallas_call(
        functools.partial(_qkv_kernel, n_qk=(q_dim + kv_dim) // bn, norm=norm),
        grid=(m // bm, n // bn),
        in_specs=[
            pl.BlockSpec((bm, d), lambda i, j: (i, 0)),
            pl.BlockSpec((None, d, bn), lambda i, j: (layer, 0, j)),
            pl.BlockSpec((1, bn), lambda i, j: (0, j)),
        ],
        out_specs=[pl.BlockSpec((bm, bn), lambda i, j: (i, j)),
                   pl.BlockSpec((bm, bn), lambda i, j: (i, jnp.maximum(j - n_q, 0)))],
        out_shape=[jax.ShapeDtypeStruct((m, n), BF16), jax.ShapeDtypeStruct((m, 2 * kv_dim), F32)],
        compiler_params=_params("parallel", "arbitrary"),
        name="qkv_proj",
    )(h, w, colscale)


def _head_epilogue(acc, cs_ref, cos, sin, is_v, norm):
    lane = lax.broadcasted_iota(jnp.int32, (1, LANE), 1)
    first_half = (lane % (LANE // 2)) < (LANE // 4)
    outs = []
    for hh in range(acc.shape[1] // LANE):
        blk = acc[:, hh * LANE:(hh + 1) * LANE]
        if norm:
            inv = lax.rsqrt(jnp.mean(blk * blk, axis=-1, keepdims=True) + EPS)
            blk = blk * jnp.where(is_v, 1.0, inv)
        blk = blk * cs_ref[:, hh * LANE:(hh + 1) * LANE]
        partner = jnp.where(first_half, pltpu.roll(blk, LANE - LANE // 4, 1),
                            pltpu.roll(blk, LANE // 4, 1))
        outs.append(blk * cos + partner * sin)
    return jnp.concatenate(outs, axis=1) if len(outs) > 1 else outs[0]


def _cast_specs(casts, steps, step_of):
    in_specs, out_specs, shapes = [], [], []
    for arr, l in casts:
        _, r, c = arr.shape
        rows = PACK * pl.cdiv(r, PACK * steps)
        blk = lambda *ids, n=pl.cdiv(r, rows): jnp.minimum(step_of(*ids), n - 1)
        in_specs.append(pl.BlockSpec((None, rows, c), lambda *ids, l=l, blk=blk: (l, blk(*ids), 0)))
        out_specs.append(pl.BlockSpec((rows, c), lambda *ids, blk=blk: (blk(*ids), 0)))
        shapes.append(jax.ShapeDtypeStruct((r, c), BF16))
    return in_specs, out_specs, shapes


def _cast_rows(cast_in, cast_out):
    for src, dst in zip(cast_in, cast_out):
        dst[...] = src[...].astype(dst.dtype)


def _qkv_lag_kernel(*refs, n_i, n_qk, norm, n_cast):
    h_ref, w_ref, cs_ref, cos_ref, sin_ref = refs[:5]
    o_ref = refs[5 + n_cast]
    acc_ref = refs[6 + 2 * n_cast]
    _cast_rows(refs[5:5 + n_cast], refs[6 + n_cast:6 + 2 * n_cast])
    s = pl.program_id(0)

    @pl.when(s == 0)
    def _():
        acc_ref[...] = jnp.zeros_like(acc_ref)

    is_v = jnp.maximum(s - 1, 0) // n_i >= n_qk
    cos = jnp.where(is_v, 1.0, cos_ref[...])
    sin = jnp.where(is_v, 0.0, sin_ref[...])
    o_ref[...] = _head_epilogue(acc_ref[...], cs_ref, cos, sin, is_v, norm).astype(o_ref.dtype)
    acc_ref[...] = _dot(h_ref[...], w_ref[...])


def _qkv_proj_rope(h, w, layer, colscale, q_dim, kv_dim, *, norm, rope_tables, casts=()):
    m, d = h.shape
    n = w.shape[2]
    bm = _tile(m, TILES["qkv_rope_m"], PACK)
    bn = _tile(kv_dim, TILES["qkv_n"], LANE)
    assert q_dim % bn == 0 and kv_dim % bn == 0
    n_i = m // bm
    steps = n_i * (n // bn)
    cos, sin = rope_tables
    per = cos.shape[0] // bm
    assert cos.shape[0] % bm == 0
    cur = lambda s: jnp.minimum(s, steps - 1)
    fin = lambda s: jnp.maximum(s - 1, 0)
    cast_in, cast_out, cast_shapes = _cast_specs(casts, steps + 1, lambda s: s)
    out = pl.pallas_call(
        functools.partial(_qkv_lag_kernel, n_i=n_i, n_qk=(q_dim + kv_dim) // bn, norm=norm,
                          n_cast=len(casts)),
        grid=(steps + 1,),
        in_specs=[
            pl.BlockSpec((bm, d), lambda s: (cur(s) % n_i, 0)),
            pl.BlockSpec((None, d, bn), lambda s: (layer, 0, cur(s) // n_i), pipeline_mode=pl.Buffered(1)),
            pl.BlockSpec((1, bn), lambda s: (0, fin(s) // n_i)),
            pl.BlockSpec((bm, LANE), lambda s: ((fin(s) % n_i) % per, 0)),
            pl.BlockSpec((bm, LANE), lambda s: ((fin(s) % n_i) % per, 0)),
        ] + cast_in,
        out_specs=[pl.BlockSpec((bm, bn), lambda s: (fin(s) % n_i, fin(s) // n_i))] + cast_out,
        out_shape=[jax.ShapeDtypeStruct((m, n), BF16)] + cast_shapes,
        scratch_shapes=[pltpu.VMEM((bm, bn), F32)],
        compiler_params=_params("arbitrary"),
        name="qkv_proj_rope",
    )(h, w, colscale, cos, sin, *[arr for arr, _ in casts])
    return out[0], list(out[1:])


def _stack_heads(q_ref, r_heads):
    return jnp.concatenate([q_ref[:, r * LANE:(r + 1) * LANE] for r in range(r_heads)], axis=0)


def _store_heads_t(o_ref, o_t, r_heads):
    tq = o_t.shape[1] // r_heads
    for r in range(r_heads):
        o_ref[:, r * LANE:(r + 1) * LANE] = o_t[:, r * tq:(r + 1) * tq].T.astype(o_ref.dtype)


def _max_key_norm(*k_refs):
    def max_sq_norm(ref):
        x = ref[...].astype(F32)
        return jnp.max(jnp.sum(x * x, axis=1, keepdims=True))
    return jnp.sqrt(functools.reduce(jnp.maximum, [max_sq_norm(r) for r in k_refs]))


def _sink_row(sink_ref, g, r_heads, tq):
    return jnp.concatenate(
        [jnp.full((1, tq), sink_ref[g * r_heads + r], F32) for r in range(r_heads)], axis=1)


def _attend(q_ref, o_ref, segments, k_max, sink, r_heads):
    qs = _stack_heads(q_ref, r_heads)
    rows = qs.shape[0]
    q_sq = _dot_nt(jnp.ones((SUB, LANE), BF16), qs * qs)[0:1]
    bound = jnp.sqrt(q_sq) * k_max
    if sink is not None:
        bound = jnp.maximum(bound, sink)

    def scores(k, bias):
        s = _dot_nt(k, qs)
        return s if bias is None else s + bias

    def attempt(off):
        den = jnp.zeros((1, rows), F32)
        acc = jnp.zeros((LANE, rows), F32)
        for k, v_t, bias in segments():
            p = jnp.exp2(scores(k, bias) - off)
            den = den + jnp.sum(p, axis=0, keepdims=True)
            acc = acc + _dot(v_t, p.astype(BF16))
        if sink is not None:
            den = den + jnp.exp2(sink - off)
        return acc * (1.0 / den), den

    out, den = attempt(bound)
    _store_heads_t(o_ref, out, r_heads)

    @pl.when(jnp.min(den) < UNDERFLOW_SUM)
    def _():
        row_max = jnp.full((1, rows), NEG_BIG, F32) if sink is None else sink
        for k, _, bias in segments():
            row_max = jnp.maximum(row_max, jnp.max(scores(k, bias), axis=0, keepdims=True))
        out, _ = attempt(row_max)
        _store_heads_t(o_ref, out, r_heads)


def _self_attn_kernel(sink_ref, q_ref, k_ref, vt_ref, o_ref, kmax_ref, *, r_heads, n_kv, has_sink):
    t = q_ref.shape[0]
    qw = r_heads * LANE
    for g in range(n_kv):
        k_g = k_ref.at[:, g * LANE:(g + 1) * LANE]
        vt_g = vt_ref.at[g * LANE:(g + 1) * LANE, :]
        kmax_ref[g] = _max_key_norm(k_g)
        sink = _sink_row(sink_ref, g, r_heads, t) if has_sink else None
        _attend(q_ref.at[:, g * qw:(g + 1) * qw], o_ref.at[:, g * qw:(g + 1) * qw],
                lambda k_g=k_g, vt_g=vt_g: [(k_g[...], vt_g[...], None)], kmax_ref[g], sink, r_heads)


def _self_attention(qkv, v_t, sink, batch, q_dim, kv_dim, *, has_sink):
    m = qkv.shape[0]
    t = m // batch
    n_kv = kv_dim // LANE
    r_heads = q_dim // kv_dim
    return pl.pallas_call(
        functools.partial(_self_attn_kernel, r_heads=r_heads, n_kv=n_kv, has_sink=has_sink),
        grid=(batch,),
        in_specs=[
            pl.BlockSpec(memory_space=pltpu.SMEM),
            pl.BlockSpec((t, q_dim), lambda b: (b, 0)),
            pl.BlockSpec((t, kv_dim), lambda b: (b, q_dim // kv_dim)),
            pl.BlockSpec((kv_dim, t), lambda b: (0, b)),
        ],
        out_specs=pl.BlockSpec((t, q_dim), lambda b: (b, 0)),
        out_shape=jax.ShapeDtypeStruct((m, q_dim), BF16),
        scratch_shapes=[pltpu.SMEM((n_kv,), F32)],
        compiler_params=_params("parallel"),
        name="self_attention",
    )(sink, qkv, qkv, v_t)


def _win_attn_kernel(sink_ref, q_ref, k_ref, vt_ref, kc_ref, vct_ref, bias_ref, o_ref,
                     kmax_ref, *, r_heads, span):
    i = pl.program_id(2)
    tq = q_ref.shape[0]
    t = k_ref.shape[0]

    @pl.when(i == 0)
    def _():
        kmax_ref[0] = _max_key_norm(k_ref, kc_ref)

    base = pl.multiple_of(jnp.clip(i * tq - WINDOW, 0, t - span), LANE)

    def segments():
        k_w = k_ref[pl.ds(base, span), :]
        v_w_t = jnp.concatenate([vt_ref[base // LANE + c] for c in range(span // LANE)], axis=1)
        bias = jnp.concatenate([bias_ref[...]] * r_heads, axis=1)
        return [(kc_ref[...], vct_ref[...], None), (k_w, v_w_t, bias)]

    _attend(q_ref, o_ref, segments, kmax_ref[0],
            _sink_row(sink_ref, pl.program_id(1), r_heads, tq), r_heads)


def _window_attention(qkv, v_t, k_ctx, v_ctx_t, layer, sink, batch, q_dim, kv_dim):
    m = qkv.shape[0]
    t = m // batch
    n_kv = kv_dim // LANE
    r_heads = q_dim // kv_dim
    qw = r_heads * LANE
    tq = _tile(t, TILES["win_q"], LANE)
    reach = -(-WINDOW // LANE) * LANE
    span = tq + 2 * reach
    assert t % tq == 0 and t >= span
    nb = t // tq
    p = k_ctx.shape[3]
    blk = jnp.array([0, min(1, nb - 1), nb - 1], dtype=jnp.int32)[:, None, None]
    base = jnp.clip(blk * tq - reach, 0, t - span)
    kpos = base + jnp.arange(span, dtype=jnp.int32)[None, :, None]
    qpos = blk * tq + jnp.arange(tq, dtype=jnp.int32)[None, None, :]
    bias = jnp.where(jnp.abs(qpos - kpos) <= WINDOW, 0.0, NEG_BIG).astype(F32)
    variant = lambda i: jnp.where(i == 0, 0, jnp.where(i == nb - 1, 2, 1))
    return pl.pallas_call(
        functools.partial(_win_attn_kernel, r_heads=r_heads, span=span),
        grid=(batch, n_kv, nb),
        in_specs=[
            pl.BlockSpec(memory_space=pltpu.SMEM),
            pl.BlockSpec((tq, qw), lambda b, g, i: (b * nb + i, g)),
            pl.BlockSpec((t, LANE), lambda b, g, i: (b, q_dim // LANE + g)),
            pl.BlockSpec((None, t // LANE, LANE, LANE), lambda b, g, i: (g, b, 0, 0)),
            pl.BlockSpec((None, None, None, p, LANE), lambda b, g, i: (b, layer, g, 0, 0)),
            pl.BlockSpec((None, None, None, LANE, p), lambda b, g, i: (b, layer, g, 0, 0)),
            pl.BlockSpec((None, span, tq), lambda b, g, i: (variant(i), 0, 0)),
        ],
        out_specs=pl.BlockSpec((tq, qw), lambda b, g, i: (b * nb + i, g)),
        out_shape=jax.ShapeDtypeStruct((m, q_dim), BF16),
        scratch_shapes=[pltpu.SMEM((1,), F32)],
        compiler_params=_params("parallel", "parallel", "arbitrary"),
        name="window_attention",
    )(sink, qkv, qkv, v_t, k_ctx, v_ctx_t, bias)


def _dense_attn_kernel(q_ref, k_ref, vt_ref, kc_ref, vct_ref, o_ref, kmax_ref, *, r_heads, kv_chunk):
    t = k_ref.shape[0]

    @pl.when(pl.program_id(2) == 0)
    def _():
        kmax_ref[0] = _max_key_norm(k_ref, kc_ref)

    def segments():
        return [(kc_ref[...], vct_ref[...], None)] + [
            (k_ref[c * kv_chunk:(c + 1) * kv_chunk, :], vt_ref[:, c * kv_chunk:(c + 1) * kv_chunk], None)
            for c in range(t // kv_chunk)]

    _attend(q_ref, o_ref, segments, kmax_ref[0], None, r_heads)


def _dense_attention(qkv, v_t, k_ctx, v_ctx_t, layer, batch, q_dim, kv_dim):
    m = qkv.shape[0]
    t = m // batch
    n_kv = kv_dim // LANE
    r_heads = q_dim // kv_dim
    qw = r_heads * LANE
    tq = _tile(t, TILES["attn_q"], LANE)
    kv_chunk = _tile(t, TILES["attn_kv"], LANE)
    nb = t // tq
    p = k_ctx.shape[3]
    return pl.pallas_call(
        functools.partial(_dense_attn_kernel, r_heads=r_heads, kv_chunk=kv_chunk),
        grid=(batch, n_kv, nb),
        in_specs=[
            pl.BlockSpec((tq, qw), lambda b, g, i: (b * nb + i, g)),
            pl.BlockSpec((t, LANE), lambda b, g, i: (b, q_dim // LANE + g)),
            pl.BlockSpec((LANE, t), lambda b, g, i: (g, b)),
            pl.BlockSpec((None, None, None, p, LANE), lambda b, g, i: (b, layer, g, 0, 0)),
            pl.BlockSpec((None, None, None, LANE, p), lambda b, g, i: (b, layer, g, 0, 0)),
        ],
        out_specs=pl.BlockSpec((tq, qw), lambda b, g, i: (b * nb + i, g)),
        out_shape=jax.ShapeDtypeStruct((m, q_dim), BF16),
        scratch_shapes=[pltpu.SMEM((1,), F32)],
        compiler_params=_params("parallel", "parallel", "arbitrary"),
        name="dense_attention",
    )(qkv, qkv, v_t, k_ctx, v_ctx_t)


def _proj_res_kernel(a_ref, w_ref, x_ref, g_ref, o_ref):
    o_ref[...] = x_ref[...] + g_ref[...] * _dot(a_ref[...], w_ref[...])


def _proj_residual(a, w, layer, x, mod, gate_slot, bm_pref, bn_pref):
    m, k = a.shape
    d = w.shape[2]
    bm = _tile(min(m, mod.rows_per_group), bm_pref, PACK)
    bn = _tile(d, bn_pref, LANE)
    nj = d // bn
    gate = mod.spec(gate_slot, bm, bn, nj)
    return pl.pallas_call(
        _proj_res_kernel,
        grid=(nj, m // bm),
        in_specs=[
            pl.BlockSpec((bm, k), lambda j, i: (i, 0)),
            pl.BlockSpec((None, k, bn), lambda j, i: (layer, 0, j)),
            pl.BlockSpec((bm, bn), lambda j, i: (i, j)),
            pl.BlockSpec(gate.block_shape, lambda j, i: gate.index_map(i, j)),
        ],
        out_specs=pl.BlockSpec((bm, bn), lambda j, i: (i, j)),
        out_shape=jax.ShapeDtypeStruct((m, d), F32),
        compiler_params=_params("parallel", "parallel"),
        name="proj_residual",
    )(a, w, x, mod.table)


def _matmul_kernel(a_ref, w_ref, o_ref):
    o_ref[...] = _dot(a_ref[...], w_ref[...])


def _edge_rows_kernel(before_ref, after_ref, o_ref):
    row = lax.broadcasted_iota(jnp.int32, o_ref.shape, 0)
    last = before_ref[...].astype(F32)[PACK - 1:PACK, :]
    first = after_ref[...].astype(F32)[0:1, :]
    o_ref[...] = jnp.where(row == SUB - 1, last, jnp.where(row == SUB, first, 0.0)).astype(o_ref.dtype)


def _edge_gate(h, wg, layer, bm):
    m, d = h.shape
    nt = m // bm
    d_ff = wg.shape[2]
    per = bm // PACK
    edge = pl.pallas_call(
        _edge_rows_kernel,
        grid=(nt,),
        in_specs=[pl.BlockSpec((PACK, d), lambda i: (jnp.maximum(i * per - 1, 0), 0)),
                  pl.BlockSpec((PACK, d), lambda i: (jnp.minimum((i + 1) * per, nt * per - 1), 0))],
        out_specs=pl.BlockSpec((None, PACK, d), lambda i: (i, 0, 0)),
        out_shape=jax.ShapeDtypeStruct((nt, PACK, d), h.dtype),
        compiler_params=_params("parallel"),
        name="edge_rows",
    )(h, h)
    bn = min(d_ff, TILES["ffn_f"])
    out = pl.pallas_call(
        _matmul_kernel,
        grid=(pl.cdiv(d_ff, bn),),
        in_specs=[pl.BlockSpec((nt * PACK, d), lambda j: (0, 0)),
                  pl.BlockSpec((None, d, bn), lambda j: (layer, 0, j))],
        out_specs=pl.BlockSpec((nt * PACK, bn), lambda j: (0, j)),
        out_shape=jax.ShapeDtypeStruct((nt * PACK, d_ff), F32),
        compiler_params=_params("parallel"),
        name="edge_gate",
    )(edge.reshape(nt * PACK, d), wg)
    return out.reshape(nt, PACK, d_ff)


def _gate_up_kernel(*refs, bm, seq_len, n_cast):
    h_ref, wg_ref, wu_ref, cw_ref, cb_ref, e_ref = refs[:6]
    cast_in = refs[6:6 + n_cast]
    o_ref = refs[6 + n_cast]
    cast_out = refs[7 + n_cast:7 + 2 * n_cast]
    a_ref = refs[7 + 2 * n_cast]
    _cast_rows(cast_in, cast_out)
    a_ref[0:SUB, :] = e_ref[0:SUB, :]
    a_ref[SUB:SUB + bm, :] = _dot(h_ref[...], wg_ref[...])
    a_ref[SUB + bm:, :] = e_ref[SUB:, :]
    u = _dot(h_ref[...], wu_ref[...])
    row = pl.program_id(0) * bm + lax.broadcasted_iota(jnp.int32, (bm, 1), 0)
    pos = row % seq_len
    prev = jnp.where(pos == 0, 0.0, a_ref[SUB - 1:SUB - 1 + bm, :])
    nxt = jnp.where(pos == seq_len - 1, 0.0, a_ref[SUB + 1:SUB + 1 + bm, :])
    cur = a_ref[SUB:SUB + bm, :]
    a = prev * cw_ref[0:1, :] + cur * cw_ref[1:2, :] + nxt * cw_ref[2:3, :] + cb_ref[...]
    o_ref[...] = (_silu(a) * u).astype(o_ref.dtype)


def _gate_up(h, wg, wu, conv_w, conv_b, layer, seq_len, bm, casts=()):
    m, d = h.shape
    (wg_arr, wg_l), (wu_arr, wu_l) = wg, wu
    d_ff = wg_arr.shape[2]
    bf = min(d_ff, TILES["ffn_f"])
    n_f = pl.cdiv(d_ff, bf)
    steps = (m // bm) * n_f
    edge = _edge_gate(h, wg_arr, wg_l, bm)
    cast_in_specs, cast_out_specs, cast_shapes = _cast_specs(casts, steps, lambda i, f: i * n_f + f)
    out = pl.pallas_call(
        functools.partial(_gate_up_kernel, bm=bm, seq_len=seq_len, n_cast=len(casts)),
        grid=(m // bm, n_f),
        in_specs=[
            pl.BlockSpec((bm, d), lambda i, f: (i, 0)),
            pl.BlockSpec((None, d, bf), lambda i, f: (wg_l, 0, f)),
            pl.BlockSpec((None, d, bf), lambda i, f: (wu_l, 0, f)),
            pl.BlockSpec((None, conv_w.shape[1], bf), lambda i, f: (layer, 0, f)),
            pl.BlockSpec((None, 1, bf), lambda i, f: (layer, 0, f)),
            pl.BlockSpec((None, PACK, bf), lambda i, f: (i, 0, f)),
        ] + cast_in_specs,
        out_specs=[pl.BlockSpec((bm, bf), lambda i, f: (i, f))] + cast_out_specs,
        out_shape=[jax.ShapeDtypeStruct((m, d_ff), BF16)] + cast_shapes,
        scratch_shapes=[pltpu.VMEM((bm + PACK, bf), F32)],
        compiler_params=_params("arbitrary", "arbitrary"),
        name="gate_up",
    )(h, wg_arr, wu_arr, conv_w, conv_b, edge, *[arr for arr, _ in casts])
    return out[0], list(out[1:])


def _rope_tables(n_tokens):
    axis_dim = LANE // 2
    half = axis_dim // 2
    rows = n_tokens // GRID_W
    row = jnp.repeat(jnp.arange(rows, dtype=jnp.int32), GRID_W)
    col = jnp.tile(jnp.arange(GRID_W, dtype=jnp.int32), rows)
    inv_freq = ROPE_THETA ** (-jnp.arange(half, dtype=F32) / half)
    cos_parts, sin_parts = [], []
    for pos in (row, col):
        ang = pos.astype(F32)[:, None] * inv_freq[None, :]
        c, s = jnp.cos(ang), jnp.sin(ang)
        cos_parts += [c, c]
        sin_parts += [-s, s]
    return jnp.concatenate(cos_parts, axis=1), jnp.concatenate(sin_parts, axis=1)


def kernel(x_prompt, x_sample, cache_k, cache_v, c, c_ctx, w_mod, b_mod, norm_attn, norm_ffn, w_qkv, w_o,
           sink_a, q_norm_b, k_norm_b, w_gate, w_up, w_down, conv_w, conv_b, norm_f):
    batch_p, seq_p, d = x_prompt.shape
    batch_s, seq_s, _ = x_sample.shape
    depth = w_mod.shape[0]
    n_kv, head_dim = cache_k.shape[3], cache_k.shape[4]
    n_heads = sink_a.shape[1]
    assert head_dim == LANE
    q_dim, kv_dim = n_heads * head_dim, n_kv * head_dim
    d_ff = w_gate.shape[2]
    scale = head_dim ** -0.5
    m_p, m_s = batch_p * seq_p, batch_s * seq_s

    ffn_m_p = _tile(m_p, TILES["ffn_m"], seq_p)
    ffn_m_s = _tile(seq_s, TILES["ffn_m"], PACK)

    xp = x_prompt.reshape(m_p, d)
    xs = x_sample.reshape(m_s, d)

    n_cond = -(-(batch_s + 1) // SUB) * SUB
    cond = jnp.zeros((n_cond, d), F32).at[:batch_s].set(c).at[batch_s].set(c_ctx)
    mod_table = _modulation(cond, w_mod, b_mod).reshape(depth * n_cond, 1, -1)

    f32_weights = dict(qkv=w_qkv, o=w_o, gate=w_gate, up=w_up, down=w_down)
    bf16 = {("qkv", 0): (w_qkv[:1].astype(BF16), 0)}
    cw = conv_w
    cb = conv_b.reshape(depth, 1, d_ff)
    gains_attn = norm_attn.reshape(depth, 1, d)
    gains_ffn = norm_ffn.reshape(depth, 1, d)
    k_ctx = cache_k.transpose(0, 1, 3, 2, 4).astype(BF16)
    v_ctx_t = cache_v.transpose(0, 1, 3, 4, 2).astype(BF16)

    rope_tables = _rope_tables(seq_s)
    ones_kv = jnp.ones((kv_dim,), F32)
    ks_out, vs_out = [], []
    for i in range(depth):
        mod_s = _Mod(mod_table, i * n_cond, seq_s)
        mod_p = _Mod(mod_table, i * n_cond + batch_s, m_p)
        mixer_a = i % 2 == 0
        if mixer_a:
            sink = sink_a[i // 2].astype(F32) * LOG2E
            colscale = jnp.concatenate([jnp.full((q_dim,), scale * LOG2E, F32), ones_kv, ones_kv])
        else:
            sink = jnp.zeros((n_heads,), F32)
            colscale = jnp.concatenate([jnp.tile(q_norm_b[i // 2].astype(F32), n_heads) * (scale * LOG2E),
                                        jnp.tile(k_norm_b[i // 2].astype(F32), n_kv), ones_kv])
        colscale = colscale.reshape(1, -1)

        hp = _norm_mod(xp, gains_attn, i, mod_p, 0)
        hs = _norm_mod(xs, gains_attn, i, mod_s, 0)
        qkv_p, kv32 = _qkv_proj(hp, *bf16["qkv", i], colscale, q_dim, kv_dim, norm=not mixer_a)
        cast_names = ["o", "gate", "up"] if i == 0 else []
        qkv_s, cast = _qkv_proj_rope(hs, *bf16["qkv", i], colscale, q_dim, kv_dim, norm=not mixer_a,
                                     rope_tables=rope_tables,
                                     casts=[(f32_weights[name], i) for name in cast_names])
        for name, w in zip(cast_names, cast):
            bf16[name, i] = (w[None], 0)
        ks_out.append(kv32[:, :kv_dim].reshape(batch_p, seq_p, n_kv, head_dim))
        vs_out.append(kv32[:, kv_dim:].reshape(batch_p, seq_p, n_kv, head_dim))
        op = _self_attention(qkv_p, qkv_p[:, q_dim + kv_dim:].T, sink, batch_p, q_dim, kv_dim,
                             has_sink=mixer_a)
        v_s = qkv_s[:, q_dim + kv_dim:]
        if mixer_a:
            v_t = v_s.reshape(m_s // LANE, LANE, n_kv, LANE).transpose(2, 0, 3, 1)
            osm = _window_attention(qkv_s, v_t, k_ctx, v_ctx_t, i, sink, batch_s, q_dim, kv_dim)
        else:
            osm = _dense_attention(qkv_s, v_s.T, k_ctx, v_ctx_t, i, batch_s, q_dim, kv_dim)
        xp = _proj_residual(op, *bf16["o", i], xp, mod_p, 2, TILES["proj_m"], TILES["proj_n"])
        xs = _proj_residual(osm, *bf16["o", i], xs, mod_s, 2, TILES["proj_m"], TILES["proj_n"])

        hp = _norm_mod(xp, gains_ffn, i, mod_p, 3)
        hs = _norm_mod(xs, gains_ffn, i, mod_s, 3)
        cast_names = ["down"] + (["qkv", "o", "gate", "up"] if i + 1 < depth else [])
        zs, cast = _gate_up(hs, bf16["gate", i], bf16["up", i], cw, cb, i, seq_s, ffn_m_s,
                            casts=[(f32_weights[name], i if name == "down" else i + 1) for name in cast_names])
        for name, w in zip(cast_names, cast):
            bf16[name, i if name == "down" else i + 1] = (w[None], 0)
        zp, _ = _gate_up(hp, bf16["gate", i], bf16["up", i], cw, cb, i, seq_p, ffn_m_p)
        xp = _proj_residual(zp, *bf16["down", i], xp, mod_p, 5, TILES["down_m"], TILES["down_n"])
        xs = _proj_residual(zs, *bf16["down", i], xs, mod_s, 5, TILES["down_m"], TILES["down_n"])

    y_prompt = _final_norm(xp, norm_f).reshape(batch_p, seq_p, d)
    y_sample = _final_norm(xs, norm_f).reshape(batch_s, seq_s, d)
    return y_prompt, y_sample, jnp.stack(ks_out, axis=1), jnp.stack(vs_out, axis=1)
```
